```python
import jax
import jax.numpy as jnp
from jax import lax
import numpy as np

D_MODEL = 1024
BATCH = 8
SEQ = 4096
DEPTH = 2

CTX_LEN = 256
GRID_W = 64
N_EVEN = (DEPTH + 1) // 2
N_ODD = DEPTH // 2
EPS = 1e-6

MLA_HEADS = 8
QK_NOPE = 64
QK_ROPE = 32
QK_HEAD = QK_NOPE + QK_ROPE
V_HEAD = 64
Q_LORA = 384
KV_LORA = 256
ROPE_BASE = 10000.0
Q_BLOCK = 128

FNET_GROUPS = 4
FNET_GROUP_W = 128
FNET_W = FNET_GROUPS * FNET_GROUP_W

CONF_GROUPS = 4
CONF_W = 512
CONF_WIDTH = 31
SC_W = 512
SC_WIDTH = 3

D_FF = -(-8 * D_MODEL // (3 * 256)) * 256

EVEN_KV_END = Q_LORA + KV_LORA + QK_ROPE
EVEN_IN = EVEN_KV_END + FNET_W
EVEN_MIX = MLA_HEADS * V_HEAD + FNET_W
ODD_IN = 2 * CONF_W + 3 * SC_W
ODD_MIX = CONF_W + SC_W

kernel_name = 'hybrid_mla_fnet_conformer_shortconv_dit'


def rms_norm(x, g):
    xf = x.astype(jnp.float32)
    y = xf * lax.rsqrt(jnp.mean(jnp.square(xf), axis=-1, keepdims=True) + EPS)
    return (y * g.astype(jnp.float32)).astype(x.dtype)


def group_layer_norm(x, g, b, groups):
    bsz, n, w = x.shape
    xf = x.astype(jnp.float32).reshape(bsz, n, groups, w // groups)
    mu = jnp.mean(xf, axis=-1, keepdims=True)
    var = jnp.mean(jnp.square(xf - mu), axis=-1, keepdims=True)
    y = ((xf - mu) * lax.rsqrt(var + EPS)).reshape(bsz, n, w)
    return (y * g.astype(jnp.float32) + b.astype(jnp.float32)).astype(x.dtype)


def modulate(h, shift, scale):
    return h * (1.0 + scale) + shift


def ada_mod(cvec, w, b):
    return jnp.split(jax.nn.silu(cvec) @ w + b, 6, axis=-1)


def axial_rope_tables(n):
    rows = n // GRID_W
    row = jnp.broadcast_to(jnp.arange(rows, dtype=jnp.float32)[:, None], (rows, GRID_W)).reshape(n)
    col = jnp.broadcast_to(jnp.arange(GRID_W, dtype=jnp.float32)[None, :], (rows, GRID_W)).reshape(n)
    per_axis = QK_ROPE // 4
    inv_freq = ROPE_BASE ** (-jnp.arange(per_axis, dtype=jnp.float32) / per_axis)
    ang = jnp.concatenate([row[:, None] * inv_freq, col[:, None] * inv_freq], axis=-1)
    return jnp.cos(ang), jnp.sin(ang)


def rope_tail(t, rope):
    if rope is None:
        return t
    cos, sin = rope
    half = QK_ROPE // 2
    r = t[..., QK_NOPE:].astype(jnp.float32)
    r1, r2 = r[..., :half], r[..., half:]
    rot = jnp.concatenate([r1 * cos - r2 * sin, r2 * cos + r1 * sin], axis=-1).astype(t.dtype)
    return jnp.concatenate([t[..., :QK_NOPE], rot], axis=-1)


def mla_queries(cq, q_ln_g, w_uq, q_norm_g, rope):
    q = jnp.einsum('bsr,rhd->bshd', rms_norm(cq, q_ln_g), w_uq)
    return rope_tail(rms_norm(q, q_norm_g), rope)


def mla_keys_values(ckv, kr, kv_ln_g, w_uk, w_uv, k_norm_g, rope):
    ckv = rms_norm(ckv, kv_ln_g)
    k_nope = jnp.einsum('bsr,rhd->bshd', ckv, w_uk)
    v = jnp.einsum('bsr,rhd->bshd', ckv, w_uv)
    k_rope = jnp.broadcast_to(kr[:, :, None, :], k_nope.shape[:-1] + (QK_ROPE,))
    k = rope_tail(rms_norm(jnp.concatenate([k_nope, k_rope], axis=-1), k_norm_g), rope)
    return k, v


def latent_attention(q, k, v, k_ctx, v_ctx):
    bsz, n, heads, dq = q.shape
    kt = jnp.concatenate([k_ctx, k], axis=1).transpose(0, 2, 1, 3)
    vt = jnp.concatenate([v_ctx, v], axis=1).transpose(0, 2, 1, 3)
    qb = q.reshape(bsz, n // Q_BLOCK, Q_BLOCK, heads, dq).transpose(1, 0, 2, 3, 4)
    scale = dq ** -0.5

    def block(qi):
        s = jnp.einsum('bqhd,bhkd->bhqk', qi, kt).astype(jnp.float32) * scale
        p = jax.nn.softmax(s, axis=-1).astype(vt.dtype)
        return jnp.einsum('bhqk,bhkd->bqhd', p, vt)

    o = lax.map(block, qb)
    return o.transpose(1, 0, 2, 3, 4).reshape(bsz, n, heads * V_HEAD)


def context_attention(q, k, v):
    bsz, n, heads, dq = q.shape
    s = jnp.einsum('bqhd,bkhd->bhqk', q, k).astype(jnp.float32) * (dq ** -0.5)
    p = jax.nn.softmax(s, axis=-1).astype(v.dtype)
    return jnp.einsum('bhqk,bkhd->bqhd', p, v).reshape(bsz, n, heads * V_HEAD)


def fourier_mix(u):
    bsz, n, _ = u.shape
    ug = u.astype(jnp.float32).reshape(bsz, n, FNET_GROUPS, FNET_GROUP_W)
    y = jnp.fft.fft2(ug, axes=(1, 3), norm='ortho').real
    return y.reshape(bsz, n, FNET_W).astype(u.dtype)


def depthwise_conv(u, w):
    width, ch = w.shape
    pad = (width - 1) // 2
    return lax.conv_general_dilated(
        u, w[:, None, :].astype(u.dtype), window_strides=(1,), padding=[(pad, pad)],
        dimension_numbers=('NWC', 'WIO', 'NWC'), feature_group_count=ch)


def odd_mixer(u, conf_dw, conf_dw_b, conf_ln_g, conf_ln_b, sc_dw):
    a, gate, sb, scc, sx = jnp.split(
        u, [CONF_W, 2 * CONF_W, 2 * CONF_W + SC_W, 2 * CONF_W + 2 * SC_W], axis=-1)
    conf = a * jax.nn.sigmoid(gate)
    conf = depthwise_conv(conf, conf_dw) + conf_dw_b
    conf = jax.nn.silu(group_layer_norm(conf, conf_ln_g, conf_ln_b, CONF_GROUPS))
    sc = sb * depthwise_conv(scc * sx, sc_dw)
    return jnp.concatenate([conf, sc], axis=-1)


def swiglu(h, w1, w3, w2):
    return (jax.nn.silu(h @ w1) * (h @ w3)) @ w2


def setup_inputs(seed: int = 0) -> dict:
    key = jax.random.key(seed)
    keys = iter(jax.random.split(key, 40))

    def normal(shape, scale):
        return jax.random.normal(next(keys), shape, jnp.float32) * scale

    def gain(shape):
        return 1.0 + 0.05 * jax.random.normal(next(keys), shape, jnp.float32)

    E, O, L, D = N_EVEN, N_ODD, DEPTH, D_MODEL
    return {
        'x': normal((BATCH, SEQ, D), 1.0),
        'c': normal((BATCH, D), 1.0),
        'ctx': normal((BATCH, CTX_LEN, D), 1.0),
        'c_ctx': normal((D,), 1.0),
        'ada_w': normal((L, D, 6 * D), 0.5 * D ** -0.5),
        'ada_b': normal((L, 6 * D), 0.02),
        'norm1_g': gain((L, D)),
        'norm2_g': gain((L, D)),
        'ffn_w1': normal((L, D, D_FF), D ** -0.5),
        'ffn_w3': normal((L, D, D_FF), D ** -0.5),
        'ffn_w2': normal((L, D_FF, D), D_FF ** -0.5),
        'a_w_in': normal((E, D, EVEN_IN), D ** -0.5),
        'a_q_ln_g': gain((E, Q_LORA)),
        'a_kv_ln_g': gain((E, KV_LORA)),
        'a_w_uq': normal((E, Q_LORA, MLA_HEADS, QK_HEAD), Q_LORA ** -0.5),
        'a_w_uk': normal((E, KV_LORA, MLA_HEADS, QK_NOPE), KV_LORA ** -0.5),
        'a_w_uv': normal((E, KV_LORA, MLA_HEADS, V_HEAD), KV_LORA ** -0.5),
        'a_q_norm_g': gain((E, QK_HEAD)),
        'a_k_norm_g': gain((E, QK_HEAD)),
        'a_w_out': normal((E, EVEN_MIX, D), EVEN_MIX ** -0.5),
        'b_w_in': normal((O, D, ODD_IN), D ** -0.5),
        'b_conf_dw': normal((O, CONF_WIDTH, CONF_W), CONF_WIDTH ** -0.5),
        'b_conf_dw_b': normal((O, CONF_W), 0.02),
        'b_conf_ln_g': gain((O, CONF_W)),
        'b_conf_ln_b': normal((O, CONF_W), 0.02),
        'b_sc_dw': normal((O, SC_WIDTH, SC_W), SC_WIDTH ** -0.5),
        'b_w_out': normal((O, ODD_MIX, D), ODD_MIX ** -0.5),
    }


def reference(x, c, ctx, c_ctx, ada_w, ada_b, norm1_g, norm2_g, ffn_w1, ffn_w3, ffn_w2,
              a_w_in, a_q_ln_g, a_kv_ln_g, a_w_uq, a_w_uk, a_w_uv, a_q_norm_g, a_k_norm_g, a_w_out,
              b_w_in, b_conf_dw, b_conf_dw_b, b_conf_ln_g, b_conf_ln_b, b_sc_dw, b_w_out):
    n = x.shape[1]
    cos, sin = axial_rope_tables(n)
    rope = (cos[:, None, :], sin[:, None, :])
    for i in range(DEPTH):
        ctx_needed = any(l % 2 == 0 for l in range(i + 1, DEPTH))
        j = i // 2
        sh1, sc1, g1, sh2, sc2, g2 = [t[:, None, :] for t in ada_mod(c, ada_w[i], ada_b[i])]
        h = modulate(rms_norm(x, norm1_g[i]), sh1, sc1)
        if i % 2 == 0 or ctx_needed:
            csh1, csc1, cg1, csh2, csc2, cg2 = ada_mod(c_ctx, ada_w[i], ada_b[i])
            hc = modulate(rms_norm(ctx, norm1_g[i]), csh1, csc1)
        if i % 2 == 0:
            w_in = a_w_in[j]
            u = h @ w_in
            ukc = hc @ w_in[:, Q_LORA:EVEN_KV_END]
            k_c, v_c = mla_keys_values(ukc[..., :KV_LORA], ukc[..., KV_LORA:], a_kv_ln_g[j],
                                       a_w_uk[j], a_w_uv[j], a_k_norm_g[j], None)
            k, v = mla_keys_values(u[..., Q_LORA:Q_LORA + KV_LORA], u[..., Q_LORA + KV_LORA:EVEN_KV_END],
                                   a_kv_ln_g[j], a_w_uk[j], a_w_uv[j], a_k_norm_g[j], rope)
            q = mla_queries(u[..., :Q_LORA], a_q_ln_g[j], a_w_uq[j], a_q_norm_g[j], rope)
            o = jnp.concatenate([latent_attention(q, k, v, k_c, v_c), fourier_mix(u[..., EVEN_KV_END:])], axis=-1)
            mix = o @ a_w_out[j]
            if ctx_needed:
                q_c = mla_queries(hc @ w_in[:, :Q_LORA], a_q_ln_g[j], a_w_uq[j], a_q_norm_g[j], None)
                oc = jnp.concatenate([context_attention(q_c, k_c, v_c),
                                      fourier_mix(hc @ w_in[:, EVEN_KV_END:])], axis=-1)
                mix_c = oc @ a_w_out[j]
        else:
            mix = odd_mixer(h @ b_w_in[j], b_conf_dw[j], b_conf_dw_b[j], b_conf_ln_g[j],
                            b_conf_ln_b[j], b_sc_dw[j]) @ b_w_out[j]
            if ctx_needed:
                mix_c = odd_mixer(hc @ b_w_in[j], b_conf_dw[j], b_conf_dw_b[j], b_conf_ln_g[j],
                                  b_conf_ln_b[j], b_sc_dw[j]) @ b_w_out[j]
        x = x + g1 * mix
        x = x + g2 * swiglu(modulate(rms_norm(x, norm2_g[i]), sh2, sc2), ffn_w1[i], ffn_w3[i], ffn_w2[i])
        if ctx_needed:
            ctx = ctx + cg1 * mix_c
            ctx = ctx + cg2 * swiglu(modulate(rms_norm(ctx, norm2_g[i]), csh2, csc2),
                                     ffn_w1[i], ffn_w3[i], ffn_w2[i])
    return x
```

```python
import functools
import math

import numpy as np
import jax
import jax.numpy as jnp
from jax import lax
from jax.experimental import pallas as pl
from jax.experimental.pallas import tpu as pltpu

F32 = jnp.float32
BF16 = jnp.bfloat16

D_MODEL = 1024
CTX_LEN = 256
GRID_W = 64
EPS = 1e-6
MLA_HEADS = 8
QK_NOPE = 64
QK_ROPE = 32
QK_HEAD = QK_NOPE + QK_ROPE
V_HEAD = 64
Q_LORA = 384
KV_LORA = 256
ROPE_BASE = 10000.0
FNET_GROUPS = 4
FNET_GROUP_W = 128
FNET_W = FNET_GROUPS * FNET_GROUP_W
CONF_GROUPS = 4
CONF_W = 512
CONF_WIDTH = 31
SC_W = 512
SC_WIDTH = 3
D_FF = 2816
EVEN_KV_END = Q_LORA + KV_LORA + QK_ROPE

LANES = 128
HEAD_PAD = LANES
ROPE_HALF = QK_ROPE // 2
FFT_R = 64
HALO = 16
VMEM_LIMIT = 56 * 1024 * 1024

TM_EVEN_IN = 512
TQ_ATTN = 512
KC_ATTN = 512
TM_OUT = 512
TM_ODD = 512
FF_CHUNKS = tuple((c, min(512, D_FF - c)) for c in range(0, D_FF, 512))

Q_SCALE = QK_HEAD ** -0.5 * math.log2(math.e)


def _dot(a, b):
    return jnp.dot(a, b, preferred_element_type=F32)


def _dot_nt(a, b):
    return lax.dot_general(a, b, (((1,), (1,)), ((), ())), preferred_element_type=F32)


def _inv_rms(x, n):
    return lax.rsqrt(jnp.sum(x * x, axis=-1, keepdims=True) * (1.0 / n) + EPS)


def _norm_mod(x, g, shift, scale1p):
    return (x * _inv_rms(x, x.shape[-1]) * g) * scale1p + shift


def _silu(a):
    return a * jax.nn.sigmoid(a)


def _const_spec(shape):
    nd = len(shape)
    return pl.BlockSpec(shape, lambda *_: (0,) * nd, pipeline_mode=pl.Buffered(1))


def _params(sem):
    return pltpu.CompilerParams(dimension_semantics=sem, vmem_limit_bytes=VMEM_LIMIT)


def _ada_kernel(c_ref, w_ref, b_ref, o_ref):
    s = _silu(c_ref[...]).astype(BF16)
    o_ref[0] = _dot(s, w_ref[0].astype(BF16)) + b_ref[0]


def _ada_call(cvec, ada_w, ada_b):
    depth, d, n6 = ada_w.shape
    rows = cvec.shape[0]
    tn = 1536
    return pl.pallas_call(
        _ada_kernel,
        grid=(depth, n6 // tn),
        in_specs=[
            pl.BlockSpec((rows, d), lambda l, j: (0, 0)),
            pl.BlockSpec((1, d, tn), lambda l, j: (l, 0, j)),
            pl.BlockSpec((1, 1, tn), lambda l, j: (l, 0, j)),
        ],
        out_specs=pl.BlockSpec((1, rows, tn), lambda l, j: (l, 0, j)),
        out_shape=jax.ShapeDtypeStruct((depth, rows, n6), F32),
        compiler_params=_params(("arbitrary", "arbitrary")),
        name="ada_mod",
    )(cvec, ada_w, ada_b.reshape(depth, 1, n6))


def _rope(t, rc, rsa, rsb):
    return t * rc + pltpu.roll(t, ROPE_HALF, 1) * rsa + pltpu.roll(t, LANES - ROPE_HALF, 1) * rsb


def _keys_values(hb, krb, rope, wkv_ref, kvlg_ref, wuk_ref, wuv_ref, gk_ref, k_ref, v_ref):
    ckv = _dot(hb, wkv_ref[...])
    ckvn = (ckv * _inv_rms(ckv, KV_LORA) * kvlg_ref[...]).astype(BF16)
    kf = _dot(ckvn, wuk_ref[...])
    v_ref[0] = _dot(ckvn, wuv_ref[...]).astype(BF16)
    gk = gk_ref[...]
    krg = krb * gk
    if rope is not None:
        krg = _rope(krg, *rope)
    ss_kr = jnp.sum(krb * krb, axis=-1, keepdims=True)
    for h in range(MLA_HEADS):
        kn = kf[:, h * HEAD_PAD:(h + 1) * HEAD_PAD]
        r = lax.rsqrt((jnp.sum(kn * kn, axis=-1, keepdims=True) + ss_kr) * (1.0 / QK_HEAD) + EPS)
        k_ref[0, :, h * HEAD_PAD:(h + 1) * HEAD_PAD] = ((kn * gk + krg) * r).astype(BF16)


def _even_in_kernel(x_ref, mod_ref, g_ref, wq_ref, wkv_ref, wf_ref, qlg_ref, kvlg_ref, wuq_ref, wuk_ref,
                    wuv_ref, gq_ref, gk_ref, rc_ref, rsa_ref, rsb_ref, cs_ref, q_ref, k_ref, v_ref, z_ref):
    hb = _norm_mod(x_ref[0], g_ref[...], mod_ref[0, 0:1, :], mod_ref[0, 1:2, :]).astype(BF16)
    rope = (rc_ref[...], rsa_ref[...], rsb_ref[...])
    uq = _dot(hb, wq_ref[...])
    _keys_values(hb, uq[:, Q_LORA:], rope, wkv_ref, kvlg_ref, wuk_ref, wuv_ref, gk_ref, k_ref, v_ref)

    cq = uq[:, :Q_LORA]
    cqn = (cq * _inv_rms(cq, Q_LORA) * qlg_ref[...]).astype(BF16)
    qf = _dot(cqn, wuq_ref[...])
    gq = gq_ref[...]
    for h in range(MLA_HEADS):
        qh = qf[:, h * HEAD_PAD:(h + 1) * HEAD_PAD]
        qn = qh * _inv_rms(qh, QK_HEAD) * gq
        q_ref[0, :, h * HEAD_PAD:(h + 1) * HEAD_PAD] = (_rope(qn, *rope) * Q_SCALE).astype(BF16)

    uf = _dot(hb, wf_ref[...]).astype(BF16)
    for g in range(FNET_GROUPS):
        zz = _dot(uf[:, g * LANES:(g + 1) * LANES], cs_ref[...])
        z_ref[0, :, g * LANES:(g + 1) * LANES] = zz[:, :LANES]
        z_ref[0, :, FNET_W + g * LANES:FNET_W + (g + 1) * LANES] = zz[:, LANES:]


def _ctx_in_kernel(x_ref, mod_ref, g_ref, wq_ref, wkv_ref, kvlg_ref, wuk_ref, wuv_ref, gk_ref, k_ref, v_ref):
    hb = _norm_mod(x_ref[0], g_ref[...], mod_ref[0, 0:1, :], mod_ref[0, 1:2, :]).astype(BF16)
    krb = _dot(hb, wq_ref[:, Q_LORA:])
    _keys_values(hb, krb, None, wkv_ref, kvlg_ref, wuk_ref, wuv_ref, gk_ref, k_ref, v_ref)


def _even_in_call(x, mod, g, w, tables):
    bsz, n, d = x.shape
    tm = TM_EVEN_IN
    hw = MLA_HEADS * HEAD_PAD
    tok = lambda width: pl.BlockSpec((1, tm, width), lambda b, i: (b, i, 0))
    tab = pl.BlockSpec((tm, LANES), lambda b, i: (i, 0))
    return pl.pallas_call(
        _even_in_kernel,
        grid=(bsz, n // tm),
        in_specs=[
            tok(d),
            pl.BlockSpec((1, 8, d), lambda b, i: (b, 0, 0)),
            _const_spec((1, d)),
            _const_spec(w["wq"].shape), _const_spec(w["wkv"].shape), _const_spec(w["wf"].shape),
            _const_spec((1, Q_LORA)), _const_spec((1, KV_LORA)),
            _const_spec(w["wuq"].shape), _const_spec(w["wuk"].shape), _const_spec(w["wuv"].shape),
            _const_spec((1, LANES)), _const_spec((1, LANES)),
            tab, tab, tab,
            _const_spec(tables["cs"].shape),
        ],
        out_specs=[tok(hw), tok(hw), tok(MLA_HEADS * V_HEAD), tok(2 * FNET_W)],
        out_shape=[
            jax.ShapeDtypeStruct((bsz, n, hw), BF16),
            jax.ShapeDtypeStruct((bsz, n, hw), BF16),
            jax.ShapeDtypeStruct((bsz, n, MLA_HEADS * V_HEAD), BF16),
            jax.ShapeDtypeStruct((bsz, n, 2 * FNET_W), F32),
        ],
        compiler_params=_params(("parallel", "parallel")),
        name="even_in",
    )(x, mod, g, w["wq"], w["wkv"], w["wf"], w["qlg"], w["kvlg"], w["wuq"], w["wuk"], w["wuv"],
      w["gq"], w["gk"], tables["rc"], tables["rsa"], tables["rsb"], tables["cs"])


def _ctx_in_call(ctx, mod, g, w):
    bsz, n, d = ctx.shape
    hw = MLA_HEADS * HEAD_PAD
    tok = lambda width: pl.BlockSpec((1, n, width), lambda b: (b, 0, 0))
    return pl.pallas_call(
        _ctx_in_kernel,
        grid=(bsz,),
        in_specs=[
            tok(d),
            pl.BlockSpec((1, 8, d), lambda b: (0, 0, 0)),
            _const_spec((1, d)),
            _const_spec(w["wq"].shape), _const_spec(w["wkv"].shape),
            _const_spec((1, KV_LORA)),
            _const_spec(w["wuk"].shape), _const_spec(w["wuv"].shape),
            _const_spec((1, LANES)),
        ],
        out_specs=[tok(hw), tok(MLA_HEADS * V_HEAD)],
        out_shape=[
            jax.ShapeDtypeStruct((bsz, n, hw), BF16),
            jax.ShapeDtypeStruct((bsz, n, MLA_HEADS * V_HEAD), BF16),
        ],
        compiler_params=_params(("parallel",)),
        name="ctx_in",
    )(ctx, mod, g, w["wq"], w["wkv"], w["kvlg"], w["wuk"], w["wuv"], w["gk"])


def _dft1_kernel(z_ref, f_ref, t_ref):
    for bi in range(8):
        z = z_ref[0, :, bi, :]
        x = jnp.concatenate([z[:, :FNET_W], z[:, FNET_W:]], axis=0).astype(BF16)
        t_ref[0, bi] = _dot(f_ref[...], x)


def _dft2_kernel(tr_ref, ti_ref, g_ref, y_ref):
    for ci in range(8):
        x = jnp.concatenate([tr_ref[0, :, ci, :], ti_ref[0, :, ci, :]], axis=0).astype(BF16)
        y_ref[0, ci] = _dot(g_ref[ci], x)


def _seq_dft_call(z, tables):
    bsz, n, w2 = z.shape
    r = FFT_R
    z4 = z.reshape(bsz, r, r, w2)
    t = pl.pallas_call(
        _dft1_kernel,
        grid=(bsz, r // 8),
        in_specs=[
            pl.BlockSpec((1, r, 8, w2), lambda b, j: (b, 0, j, 0)),
            _const_spec((2 * r, 2 * r)),
        ],
        out_specs=pl.BlockSpec((1, 8, 2 * r, FNET_W), lambda b, j: (b, j, 0, 0)),
        out_shape=jax.ShapeDtypeStruct((bsz, r, 2 * r, FNET_W), F32),
        compiler_params=_params(("parallel", "parallel")),
        name="seq_dft1",
    )(z4, tables["f1"])
    return pl.pallas_call(
        _dft2_kernel,
        grid=(r // 8, bsz),
        in_specs=[
            pl.BlockSpec((1, r, 8, FNET_W), lambda j, b: (b, 0, j, 0)),
            pl.BlockSpec((1, r, 8, FNET_W), lambda j, b: (b, 0, r // 8 + j, 0)),
            pl.BlockSpec((8, r, 2 * r), lambda j, b: (j, 0, 0)),
        ],
        out_specs=pl.BlockSpec((1, 8, r, FNET_W), lambda j, b: (b, j, 0, 0)),
        out_shape=jax.ShapeDtypeStruct((bsz, r, r, FNET_W), F32),
        compiler_params=_params(("parallel", "parallel")),
        name="seq_dft2",
    )(t, t, tables["g2"])


def _attn_kernel(q_ref, kc_ref, k_ref, vc_ref, v_ref, o_ref, s0_ref, s1_ref):
    n_ctx = kc_ref.shape[1]
    n_chunks = k_ref.shape[1] // KC_ATTN
    outs = []
    for hh, s_ref in enumerate((s0_ref, s1_ref)):
        lo = hh * HEAD_PAD
        q = q_ref[0, :, lo:lo + HEAD_PAD]
        s = _dot_nt(q, kc_ref[0, :, lo:lo + HEAD_PAD])
        s_ref[:, 0:n_ctx] = s
        m = s[:, 0:LANES]
        for t in range(1, n_ctx // LANES):
            m = jnp.maximum(m, s[:, t * LANES:(t + 1) * LANES])
        for j in range(n_chunks):
            s = _dot_nt(q, k_ref[0, j * KC_ATTN:(j + 1) * KC_ATTN, lo:lo + HEAD_PAD])
            s_ref[:, n_ctx + j * KC_ATTN:n_ctx + (j + 1) * KC_ATTN] = s
            for t in range(KC_ATTN // LANES):
                m = jnp.maximum(m, s[:, t * LANES:(t + 1) * LANES])
        mrow = jnp.max(m, axis=-1, keepdims=True)

        p = jnp.exp2(s_ref[:, 0:n_ctx] - mrow)
        l = p[:, 0:LANES]
        for t in range(1, n_ctx // LANES):
            l = l + p[:, t * LANES:(t + 1) * LANES]
        acc = _dot(p.astype(BF16), vc_ref[0])
        for j in range(n_chunks):
            p = jnp.exp2(s_ref[:, n_ctx + j * KC_ATTN:n_ctx + (j + 1) * KC_ATTN] - mrow)
            for t in range(KC_ATTN // LANES):
                l = l + p[:, t * LANES:(t + 1) * LANES]
            acc = acc + _dot(p.astype(BF16), v_ref[0, j * KC_ATTN:(j + 1) * KC_ATTN, :])
        outs.append(acc / jnp.sum(l, axis=-1, keepdims=True))
    lane = lax.broadcasted_iota(jnp.int32, outs[0].shape, 1)
    o_ref[0] = jnp.where(lane < V_HEAD, outs[0], outs[1]).astype(BF16)


def _attn_call(q, k_ctx, k, v_ctx, v):
    bsz, n, hw = q.shape
    n_ctx = k_ctx.shape[1]
    tq = TQ_ATTN
    pair = 2 * HEAD_PAD
    return pl.pallas_call(
        _attn_kernel,
        grid=(bsz, MLA_HEADS // 2, n // tq),
        in_specs=[
            pl.BlockSpec((1, tq, pair), lambda b, h, i: (b, i, h)),
            pl.BlockSpec((1, n_ctx, pair), lambda b, h, i: (b, 0, h)),
            pl.BlockSpec((1, n, pair), lambda b, h, i: (b, 0, h)),
            pl.BlockSpec((1, n_ctx, 2 * V_HEAD), lambda b, h, i: (b, 0, h)),
            pl.BlockSpec((1, n, 2 * V_HEAD), lambda b, h, i: (b, 0, h)),
        ],
        out_specs=pl.BlockSpec((1, tq, 2 * V_HEAD), lambda b, h, i: (b, i, h)),
        out_shape=jax.ShapeDtypeStruct((bsz, n, MLA_HEADS * V_HEAD), BF16),
        scratch_shapes=[pltpu.VMEM((tq, n_ctx + n), F32), pltpu.VMEM((tq, n_ctx + n), F32)],
        compiler_params=_params(("parallel", "parallel", "arbitrary")),
        name="attention",
    )(q, k_ctx, k, v_ctx, v)


def _ffn_tail(x1, mod_ref, g2n_ref, w1_ref, w3_ref, w2_ref, out_ref):
    h2 = _norm_mod(x1, g2n_ref[...], mod_ref[0, 3:4, :], mod_ref[0, 4:5, :]).astype(BF16)
    acc = None
    for c0, cw in FF_CHUNKS:
        a = _dot(h2, w1_ref[:, c0:c0 + cw])
        b = _dot(h2, w3_ref[:, c0:c0 + cw])
        part = _dot((_silu(a) * b).astype(BF16), w2_ref[c0:c0 + cw, :])
        acc = part if acc is None else acc + part
    out_ref[0] = x1 + mod_ref[0, 5:6, :] * acc


def _even_out_kernel(x_ref, o_ref, y_ref, mod_ref, g2n_ref, wo_ref, w1_ref, w3_ref, w2_ref, out_ref, ybuf):
    for di in range(TM_OUT // FFT_R):
        ybuf[di * FFT_R:(di + 1) * FFT_R, :] = y_ref[0, :, di, :].astype(BF16)
    hv = MLA_HEADS * V_HEAD
    mix = _dot(o_ref[0], wo_ref[0:hv, :]) + _dot(ybuf[...], wo_ref[hv:, :])
    x1 = x_ref[0] + mod_ref[0, 2:3, :] * mix
    _ffn_tail(x1, mod_ref, g2n_ref, w1_ref, w3_ref, w2_ref, out_ref)


def _even_out_call(x, o, y, mod, g2n, wo, w1, w3, w2):
    bsz, n, d = x.shape
    tm = TM_OUT
    r = FFT_R
    tok = lambda width: pl.BlockSpec((1, tm, width), lambda b, i: (b, i, 0))
    return pl.pallas_call(
        _even_out_kernel,
        grid=(bsz, n // tm),
        in_specs=[
            tok(d), tok(o.shape[-1]),
            pl.BlockSpec((1, r, tm // r, FNET_W), lambda b, i: (b, 0, i, 0)),
            pl.BlockSpec((1, 8, d), lambda b, i: (b, 0, 0)),
            _const_spec((1, d)),
            _const_spec(wo.shape), _const_spec(w1.shape), _const_spec(w3.shape), _const_spec(w2.shape),
        ],
        out_specs=tok(d),
        out_shape=jax.ShapeDtypeStruct((bsz, n, d), F32),
        scratch_shapes=[pltpu.VMEM((tm, FNET_W), BF16)],
        compiler_params=_params(("parallel", "parallel")),
        name="even_out_ffn",
    )(x, o, y, mod, g2n, wo, w1, w3, w2)


CONV_RB = 64


def _odd_kernel(xp_ref, x_ref, xn_ref, mod_ref, g1n_ref, win_ref, dw_ref, dwb_ref, lng_ref, lnb_ref, scw_ref,
                wout_ref, g2n_ref, w1_ref, w3_ref, w2_ref, out_ref, hb_ref, cbuf, sbuf, mbuf):
    tm = x_ref.shape[1]
    i = pl.program_id(1)
    last = pl.num_programs(1) - 1
    g = g1n_ref[...]
    sh = mod_ref[0, 0:1, :]
    sc = mod_ref[0, 1:2, :]
    x = x_ref[0]
    hb_ref[0:HALO, :] = _norm_mod(xp_ref[0], g, sh, sc).astype(BF16)
    hb_ref[HALO:HALO + tm, :] = _norm_mod(x, g, sh, sc).astype(BF16)
    hb_ref[HALO + tm:, :] = _norm_mod(xn_ref[0], g, sh, sc).astype(BF16)
    u = _dot(hb_ref[...], win_ref[...])

    conf = u[:, 0:CONF_W] * jax.nn.sigmoid(u[:, CONF_W:2 * CONF_W])
    cx = u[:, 2 * CONF_W + SC_W:2 * CONF_W + 2 * SC_W] * u[:, 2 * CONF_W + 2 * SC_W:]
    keep_lo = i > 0
    keep_hi = i < last
    cbuf[0:HALO, :] = jnp.where(keep_lo, conf[0:HALO], 0.0)
    cbuf[HALO:HALO + tm, :] = conf[HALO:HALO + tm]
    cbuf[HALO + tm:, :] = jnp.where(keep_hi, conf[HALO + tm:], 0.0)
    sbuf[0:HALO, :] = jnp.where(keep_lo, cx[0:HALO], 0.0)
    sbuf[HALO:HALO + tm, :] = cx[HALO:HALO + tm]
    sbuf[HALO + tm:, :] = jnp.where(keep_hi, cx[HALO + tm:], 0.0)
    sb = u[HALO:HALO + tm, 2 * CONF_W:2 * CONF_W + SC_W]

    pad_c = (CONF_WIDTH - 1) // 2
    pad_s = (SC_WIDTH - 1) // 2
    for rb in range(tm // CONV_RB):
        r0 = rb * CONV_RB
        for lg in range(CONF_W // LANES):
            ls = slice(lg * LANES, (lg + 1) * LANES)
            acc = jnp.broadcast_to(dwb_ref[:, ls], (CONV_RB, LANES))
            for j in range(CONF_WIDTH):
                o = HALO + r0 + j - pad_c
                acc = acc + dw_ref[j:j + 1, ls] * cbuf[o:o + CONV_RB, ls]
            mu = jnp.mean(acc, axis=-1, keepdims=True)
            cen = acc - mu
            var = jnp.mean(cen * cen, axis=-1, keepdims=True)
            yn = cen * lax.rsqrt(var + EPS) * lng_ref[:, ls] + lnb_ref[:, ls]
            mbuf[r0:r0 + CONV_RB, ls] = _silu(yn).astype(BF16)

            acc = None
            for j in range(SC_WIDTH):
                o = HALO + r0 + j - pad_s
                term = scw_ref[j:j + 1, ls] * sbuf[o:o + CONV_RB, ls]
                acc = term if acc is None else acc + term
            mbuf[r0:r0 + CONV_RB, CONF_W + lg * LANES:CONF_W + (lg + 1) * LANES] = (
                sb[r0:r0 + CONV_RB, ls] * acc).astype(BF16)

    mix = _dot(mbuf[...], wout_ref[...])
    x1 = x + mod_ref[0, 2:3, :] * mix
    _ffn_tail(x1, mod_ref, g2n_ref, w1_ref, w3_ref, w2_ref, out_ref)


def _odd_call(x, mod, g1n, win, dw, dwb, lng, lnb, scw, wout, g2n, w1, w3, w2):
    bsz, n, d = x.shape
    tm = TM_ODD
    hb = tm // HALO
    nh = n // HALO
    return pl.pallas_call(
        _odd_kernel,
        grid=(bsz, n // tm),
        in_specs=[
            pl.BlockSpec((1, HALO, d), lambda b, i: (b, jnp.maximum(i * hb - 1, 0), 0)),
            pl.BlockSpec((1, tm, d), lambda b, i: (b, i, 0)),
            pl.BlockSpec((1, HALO, d), lambda b, i: (b, jnp.minimum((i + 1) * hb, nh - 1), 0)),
            pl.BlockSpec((1, 8, d), lambda b, i: (b, 0, 0)),
            _const_spec((1, d)),
            _const_spec(win.shape),
            _const_spec(dw.shape), _const_spec(dwb.shape), _const_spec(lng.shape), _const_spec(lnb.shape),
            _const_spec(scw.shape),
            _const_spec(wout.shape),
            _const_spec((1, d)),
            _const_spec(w1.shape), _const_spec(w3.shape), _const_spec(w2.shape),
        ],
        out_specs=pl.BlockSpec((1, tm, d), lambda b, i: (b, i, 0)),
        out_shape=jax.ShapeDtypeStruct((bsz, n, d), F32),
        scratch_shapes=[
            pltpu.VMEM((tm + 2 * HALO, d), BF16),
            pltpu.VMEM((tm + 2 * HALO, CONF_W), F32),
            pltpu.VMEM((tm + 2 * HALO, SC_W), F32),
            pltpu.VMEM((tm, CONF_W + SC_W), BF16),
        ],
        compiler_params=_params(("parallel", "arbitrary")),
        name="odd_layer",
    )(x, x, x, mod, g1n, win, dw, dwb, lng, lnb, scw, wout, g2n, w1, w3, w2)


def _dft_tables():
    r = FFT_R
    gw = FNET_GROUP_W
    idx = np.arange(gw)
    ang = 2.0 * np.pi * ((idx[:, None] * idx[None, :]) % gw) / gw
    cs = np.concatenate([np.cos(ang), -np.sin(ang)], axis=1) / np.sqrt(gw)
    a = np.arange(r)
    ang1 = 2.0 * np.pi * ((a[:, None] * a[None, :]) % r) / r
    fr, fi = np.cos(ang1) / 8.0, -np.sin(ang1) / 8.0
    f1 = np.block([[fr, -fi], [fi, fr]])
    n = r * r
    c = a[:, None, None]
    dd = a[None, :, None]
    bb = a[None, None, :]
    ang2 = 2.0 * np.pi * ((bb * (c + r * dd)) % n) / n
    g2 = np.concatenate([np.cos(ang2), np.sin(ang2)], axis=2) / 8.0
    return {name: jnp.asarray(t, F32).astype(BF16) for name, t in (("cs", cs), ("f1", f1), ("g2", g2))}


def _rope_tables(n):
    rows = n // GRID_W
    row = jnp.broadcast_to(jnp.arange(rows, dtype=F32)[:, None], (rows, GRID_W)).reshape(n)
    col = jnp.broadcast_to(jnp.arange(GRID_W, dtype=F32)[None, :], (rows, GRID_W)).reshape(n)
    per_axis = QK_ROPE // 4
    inv_freq = ROPE_BASE ** (-jnp.arange(per_axis, dtype=F32) / per_axis)
    ang = jnp.concatenate([row[:, None] * inv_freq, col[:, None] * inv_freq], axis=-1)
    cos, sin = jnp.cos(ang), jnp.sin(ang)
    one = jnp.ones((n, QK_NOPE), F32)
    zn = jnp.zeros((n, QK_NOPE), F32)
    zh = jnp.zeros((n, ROPE_HALF), F32)
    tail = LANES - QK_HEAD
    rc = jnp.concatenate([one, cos, cos, jnp.ones((n, tail), F32)], axis=-1)
    rsa = jnp.concatenate([zn, zh, sin, jnp.zeros((n, tail), F32)], axis=-1)
    rsb = jnp.concatenate([zn, -sin, zh, jnp.zeros((n, tail), F32)], axis=-1)
    return rc, rsa, rsb


def _pad_heads(w, width):
    r = w.shape[0]
    return jnp.pad(w, ((0, 0), (0, 0), (0, HEAD_PAD - width))).reshape(r, MLA_HEADS * HEAD_PAD)


def _pad_gain(g):
    return jnp.pad(g, (0, HEAD_PAD - QK_HEAD)).reshape(1, HEAD_PAD)


def kernel(x, c, ctx, c_ctx, ada_w, ada_b, norm1_g, norm2_g, ffn_w1, ffn_w3, ffn_w2, a_w_in, a_q_ln_g, a_kv_ln_g, a_w_uq, a_w_uk, a_w_uv, a_q_norm_g, a_k_norm_g, a_w_out, b_w_in, b_conf_dw, b_conf_dw_b, b_conf_ln_g, b_conf_ln_b, b_sc_dw, b_w_out):
    bsz, n, d = x.shape
    depth = ada_w.shape[0]
    assert depth == 2 and d == D_MODEL and n == FFT_R * FFT_R and ctx.shape[1] == CTX_LEN

    rows = 16
    cvec = jnp.concatenate([c, c_ctx[None, :], jnp.zeros((rows - bsz - 1, d), F32)], axis=0)
    ada = _ada_call(cvec, ada_w, ada_b).reshape(depth, rows, 6, d)
    unit = jnp.array([0.0, 1.0, 0.0, 0.0, 1.0, 0.0], F32)[None, None, :, None]
    mods = jnp.pad(ada + unit, ((0, 0), (0, 0), (0, 2), (0, 0)))

    tables = _dft_tables()
    tables["rc"], tables["rsa"], tables["rsb"] = _rope_tables(n)

    w_in = a_w_in[0]
    kr_cols = jnp.pad(w_in[:, Q_LORA + KV_LORA:EVEN_KV_END], ((0, 0), (QK_NOPE, LANES - QK_HEAD)))
    w = {
        "wq": jnp.concatenate([w_in[:, :Q_LORA], kr_cols], axis=1).astype(BF16),
        "wkv": w_in[:, Q_LORA:Q_LORA + KV_LORA].astype(BF16),
        "wf": w_in[:, EVEN_KV_END:].astype(BF16),
        "qlg": a_q_ln_g[0].reshape(1, Q_LORA),
        "kvlg": a_kv_ln_g[0].reshape(1, KV_LORA),
        "wuq": _pad_heads(a_w_uq[0], QK_HEAD).astype(BF16),
        "wuk": _pad_heads(a_w_uk[0], QK_NOPE).astype(BF16),
        "wuv": a_w_uv[0].reshape(KV_LORA, MLA_HEADS * V_HEAD).astype(BF16),
        "gq": _pad_gain(a_q_norm_g[0]),
        "gk": _pad_gain(a_k_norm_g[0]),
    }
    g1 = norm1_g[0].reshape(1, d)
    q, k, v, z = _even_in_call(x, mods[0, :bsz], g1, w, tables)
    k_ctx, v_ctx = _ctx_in_call(ctx, mods[0, bsz:bsz + 1], g1, w)
    y = _seq_dft_call(z, tables)
    o = _attn_call(q, k_ctx, k, v_ctx, v)
    x = _even_out_call(x, o, y, mods[0, :bsz], norm2_g[0].reshape(1, d), a_w_out[0].astype(BF16),
                       ffn_w1[0].astype(BF16), ffn_w3[0].astype(BF16), ffn_w2[0].astype(BF16))

    x = _odd_call(x, mods[1, :bsz], norm1_g[1].reshape(1, d), b_w_in[0].astype(BF16),
                  b_conf_dw[0], b_conf_dw_b[0].reshape(1, CONF_W), b_conf_ln_g[0].reshape(1, CONF_W),
                  b_conf_ln_b[0].reshape(1, CONF_W), b_sc_dw[0], b_w_out[0].astype(BF16),
                  norm2_g[1].reshape(1, d), ffn_w1[1].astype(BF16), ffn_w3[1].astype(BF16),
                  ffn_w2[1].astype(BF16))
    return x
```

```python
import functools
import math

import numpy as np
import jax
import jax.numpy as jnp
from jax import lax
from jax.experimental import pallas as pl
from jax.experimental.pallas import tpu as pltpu

F32 = jnp.float32
BF16 = jnp.bfloat16

D_MODEL = 1024
CTX_LEN = 256
GRID_W = 64
EPS = 1e-6
MLA_HEADS = 8
QK_NOPE = 64
QK_ROPE = 32
QK_HEAD = QK_NOPE + QK_ROPE
V_HEAD = 64
Q_LORA = 384
KV_LORA = 256
ROPE_BASE = 10000.0
FNET_GROUPS = 4
FNET_GROUP_W = 128
FNET_W = FNET_GROUPS * FNET_GROUP_W
CONF_GROUPS = 4
CONF_W = 512
CONF_WIDTH = 31
SC_W = 512
SC_WIDTH = 3
D_FF = 2816
EVEN_KV_END = Q_LORA + KV_LORA + QK_ROPE

LANES = 128
HEAD_PAD = LANES
ROPE_HALF = QK_ROPE // 2
FFT_R = 64
HALO = 16
VMEM_LIMIT = 56 * 1024 * 1024

TM_EVEN_IN = 512
TQ_ATTN = 512
KC_ATTN = 512
TM_OUT = 512
TM_ODD = 512
FF_CHUNKS = tuple((c, min(512, D_FF - c)) for c in range(0, D_FF, 512))

Q_SCALE = QK_HEAD ** -0.5 * math.log2(math.e)


def _dot(a, b):
    return jnp.dot(a, b, preferred_element_type=F32)


def _dot_nt(a, b):
    return lax.dot_general(a, b, (((1,), (1,)), ((), ())), preferred_element_type=F32)


def _inv_rms(x, n):
    return lax.rsqrt(jnp.sum(x * x, axis=-1, keepdims=True) * (1.0 / n) + EPS)


def _norm_mod(x, g, shift, scale1p):
    return (x * _inv_rms(x, x.shape[-1]) * g) * scale1p + shift


def _silu(a):
    return a * jax.nn.sigmoid(a)


def _const_spec(shape):
    nd = len(shape)
    return pl.BlockSpec(shape, lambda *_: (0,) * nd, pipeline_mode=pl.Buffered(1))


def _params(sem, flags=None):
    return pltpu.CompilerParams(dimension_semantics=sem, vmem_limit_bytes=VMEM_LIMIT, flags=flags)


def _ada_kernel(c_ref, w_ref, b_ref, o_ref):
    s = _silu(c_ref[...]).astype(BF16)
    o_ref[0] = _dot(s, w_ref[0].astype(BF16)) + b_ref[0]


def _ada_call(cvec, ada_w, ada_b):
    depth, d, n6 = ada_w.shape
    rows = cvec.shape[0]
    tn = 1536
    return pl.pallas_call(
        _ada_kernel,
        grid=(depth, n6 // tn),
        in_specs=[
            pl.BlockSpec((rows, d), lambda l, j: (0, 0)),
            pl.BlockSpec((1, d, tn), lambda l, j: (l, 0, j)),
            pl.BlockSpec((1, 1, tn), lambda l, j: (l, 0, j)),
        ],
        out_specs=pl.BlockSpec((1, rows, tn), lambda l, j: (l, 0, j)),
        out_shape=jax.ShapeDtypeStruct((depth, rows, n6), F32),
        compiler_params=_params(("arbitrary", "arbitrary")),
        name="ada_mod",
    )(cvec, ada_w, ada_b.reshape(depth, 1, n6))


def _rope(t, rc, rsa, rsb):
    return t * rc + pltpu.roll(t, ROPE_HALF, 1) * rsa + pltpu.roll(t, LANES - ROPE_HALF, 1) * rsb


def _keys_values(hb, krb, rope, wkv_ref, kvlg_ref, wuk_ref, wuv_ref, gk_ref, k_ref, v_ref):
    ckv = _dot(hb, wkv_ref[...])
    ckvn = (ckv * _inv_rms(ckv, KV_LORA) * kvlg_ref[...]).astype(BF16)
    kf = _dot(ckvn, wuk_ref[...])
    v_ref[0] = _dot(ckvn, wuv_ref[...]).astype(BF16)
    gk = gk_ref[...]
    krg = krb * gk
    if rope is not None:
        krg = _rope(krg, *rope)
    ss_kr = jnp.sum(krb * krb, axis=-1, keepdims=True)
    for h in range(MLA_HEADS):
        kn = kf[:, h * HEAD_PAD:(h + 1) * HEAD_PAD]
        r = lax.rsqrt((jnp.sum(kn * kn, axis=-1, keepdims=True) + ss_kr) * (1.0 / QK_HEAD) + EPS)
        k_ref[0, :, h * HEAD_PAD:(h + 1) * HEAD_PAD] = ((kn * gk + krg) * r).astype(BF16)


def _even_in_kernel(x_ref, mod_ref, g_ref, wq_ref, wkv_ref, wf_ref, qlg_ref, kvlg_ref, wuq_ref, wuk_ref,
                    wuv_ref, gq_ref, gk_ref, rc_ref, rsa_ref, rsb_ref, cs_ref, q_ref, k_ref, v_ref, z_ref):
    hb = _norm_mod(x_ref[0], g_ref[...], mod_ref[0, 0:1, :], mod_ref[0, 1:2, :]).astype(BF16)
    rope = (rc_ref[...], rsa_ref[...], rsb_ref[...])
    uq = _dot(hb, wq_ref[...])
    _keys_values(hb, uq[:, Q_LORA:], rope, wkv_ref, kvlg_ref, wuk_ref, wuv_ref, gk_ref, k_ref, v_ref)

    cq = uq[:, :Q_LORA]
    cqn = (cq * _inv_rms(cq, Q_LORA) * qlg_ref[...]).astype(BF16)
    qf = _dot(cqn, wuq_ref[...])
    gq = gq_ref[...]
    for h in range(MLA_HEADS):
        qh = qf[:, h * HEAD_PAD:(h + 1) * HEAD_PAD]
        qn = qh * _inv_rms(qh, QK_HEAD) * gq
        q_ref[0, :, h * HEAD_PAD:(h + 1) * HEAD_PAD] = (_rope(qn, *rope) * Q_SCALE).astype(BF16)

    uf = _dot(hb, wf_ref[...]).astype(BF16)
    for g in range(FNET_GROUPS):
        zz = _dot(uf[:, g * LANES:(g + 1) * LANES], cs_ref[...])
        z_ref[0, :, g * LANES:(g + 1) * LANES] = zz[:, :LANES]
        z_ref[0, :, FNET_W + g * LANES:FNET_W + (g + 1) * LANES] = zz[:, LANES:]


def _ctx_in_kernel(x_ref, mod_ref, g_ref, wq_ref, wkv_ref, kvlg_ref, wuk_ref, wuv_ref, gk_ref, k_ref, v_ref):
    hb = _norm_mod(x_ref[0], g_ref[...], mod_ref[0, 0:1, :], mod_ref[0, 1:2, :]).astype(BF16)
    krb = _dot(hb, wq_ref[:, Q_LORA:])
    _keys_values(hb, krb, None, wkv_ref, kvlg_ref, wuk_ref, wuv_ref, gk_ref, k_ref, v_ref)


def _even_in_call(x, mod, g, w, tables):
    bsz, n, d = x.shape
    tm = TM_EVEN_IN
    hw = MLA_HEADS * HEAD_PAD
    tok = lambda width: pl.BlockSpec((1, tm, width), lambda b, i: (b, i, 0))
    tab = pl.BlockSpec((tm, LANES), lambda b, i: (i, 0))
    return pl.pallas_call(
        _even_in_kernel,
        grid=(bsz, n // tm),
        in_specs=[
            tok(d),
            pl.BlockSpec((1, 8, d), lambda b, i: (b, 0, 0)),
            _const_spec((1, d)),
            _const_spec(w["wq"].shape), _const_spec(w["wkv"].shape), _const_spec(w["wf"].shape),
            _const_spec((1, Q_LORA)), _const_spec((1, KV_LORA)),
            _const_spec(w["wuq"].shape), _const_spec(w["wuk"].shape), _const_spec(w["wuv"].shape),
            _const_spec((1, LANES)), _const_spec((1, LANES)),
            tab, tab, tab,
            _const_spec(tables["cs"].shape),
        ],
        out_specs=[tok(hw), tok(hw), tok(MLA_HEADS * V_HEAD), tok(2 * FNET_W)],
        out_shape=[
            jax.ShapeDtypeStruct((bsz, n, hw), BF16),
            jax.ShapeDtypeStruct((bsz, n, hw), BF16),
            jax.ShapeDtypeStruct((bsz, n, MLA_HEADS * V_HEAD), BF16),
            jax.ShapeDtypeStruct((bsz, n, 2 * FNET_W), F32),
        ],
        compiler_params=_params(("parallel", "parallel")),
        name="even_in",
    )(x, mod, g, w["wq"], w["wkv"], w["wf"], w["qlg"], w["kvlg"], w["wuq"], w["wuk"], w["wuv"],
      w["gq"], w["gk"], tables["rc"], tables["rsa"], tables["rsb"], tables["cs"])


def _ctx_in_call(ctx, mod, g, w):
    bsz, n, d = ctx.shape
    hw = MLA_HEADS * HEAD_PAD
    tok = lambda width: pl.BlockSpec((1, n, width), lambda b: (b, 0, 0))
    return pl.pallas_call(
        _ctx_in_kernel,
        grid=(bsz,),
        in_specs=[
            tok(d),
            pl.BlockSpec((1, 8, d), lambda b: (0, 0, 0)),
            _const_spec((1, d)),
            _const_spec(w["wq"].shape), _const_spec(w["wkv"].shape),
            _const_spec((1, KV_LORA)),
            _const_spec(w["wuk"].shape), _const_spec(w["wuv"].shape),
            _const_spec((1, LANES)),
        ],
        out_specs=[tok(hw), tok(MLA_HEADS * V_HEAD)],
        out_shape=[
            jax.ShapeDtypeStruct((bsz, n, hw), BF16),
            jax.ShapeDtypeStruct((bsz, n, MLA_HEADS * V_HEAD), BF16),
        ],
        compiler_params=_params(("parallel",)),
        name="ctx_in",
    )(ctx, mod, g, w["wq"], w["wkv"], w["kvlg"], w["wuk"], w["wuv"], w["gk"])


def _dft1_kernel(z_ref, f_ref, t_ref):
    for bi in range(8):
        z = z_ref[0, :, bi, :]
        x = jnp.concatenate([z[:, :FNET_W], z[:, FNET_W:]], axis=0).astype(BF16)
        t_ref[0, bi] = _dot(f_ref[...], x)


def _dft2_kernel(tr_ref, ti_ref, g_ref, y_ref):
    for ci in range(8):
        x = jnp.concatenate([tr_ref[0, :, ci, :], ti_ref[0, :, ci, :]], axis=0).astype(BF16)
        y_ref[0, ci] = _dot(g_ref[ci], x)


def _seq_dft_call(z, tables):
    bsz, n, w2 = z.shape
    r = FFT_R
    z4 = z.reshape(bsz, r, r, w2)
    t = pl.pallas_call(
        _dft1_kernel,
        grid=(bsz, r // 8),
        in_specs=[
            pl.BlockSpec((1, r, 8, w2), lambda b, j: (b, 0, j, 0)),
            _const_spec((2 * r, 2 * r)),
        ],
        out_specs=pl.BlockSpec((1, 8, 2 * r, FNET_W), lambda b, j: (b, j, 0, 0)),
        out_shape=jax.ShapeDtypeStruct((bsz, r, 2 * r, FNET_W), F32),
        compiler_params=_params(("parallel", "parallel")),
        name="seq_dft1",
    )(z4, tables["f1"])
    return pl.pallas_call(
        _dft2_kernel,
        grid=(r // 8, bsz),
        in_specs=[
            pl.BlockSpec((1, r, 8, FNET_W), lambda j, b: (b, 0, j, 0)),
            pl.BlockSpec((1, r, 8, FNET_W), lambda j, b: (b, 0, r // 8 + j, 0)),
            pl.BlockSpec((8, r, 2 * r), lambda j, b: (j, 0, 0)),
        ],
        out_specs=pl.BlockSpec((1, 8, r, FNET_W), lambda j, b: (b, j, 0, 0)),
        out_shape=jax.ShapeDtypeStruct((bsz, r, r, FNET_W), F32),
        compiler_params=_params(("parallel", "parallel")),
        name="seq_dft2",
    )(t, t, tables["g2"])


def _attn_kernel(q_ref, kc_ref, k_ref, vc_ref, v_ref, o_ref, s0_ref, s1_ref, m0_ref, m1_ref):
    n_ctx = kc_ref.shape[1]
    n_chunks = k_ref.shape[1] // KC_ATTN
    tq = TQ_ATTN
    n_tiles = q_ref.shape[1] // tq
    s_refs = (s0_ref, s1_ref)
    m_refs = (m0_ref, m1_ref)
    chunks = [(kc_ref, vc_ref, 0, n_ctx, 0)] + [
        (k_ref, v_ref, j * KC_ATTN, KC_ATTN, n_ctx + j * KC_ATTN) for j in range(n_chunks)]

    def lane_groups(x):
        return [x[:, t * LANES:(t + 1) * LANES] for t in range(x.shape[1] // LANES)]

    def segment(i, h_scores, h_values):
        lo = h_scores * HEAD_PAD
        q = q_ref[0, pl.ds(pl.multiple_of(i * tq, tq), tq), lo:lo + HEAD_PAD]
        s_out = s_refs[h_scores]
        m_run = acc = l = None
        if h_values is not None:
            s_in = s_refs[h_values]
            m_in = m_refs[h_values][...]
        for kr, vr, r0, width, c0 in chunks:
            if h_values is not None:
                p = [jnp.exp2(g - m_in) for g in lane_groups(s_in[:, c0:c0 + width])]
                pv = _dot(jnp.concatenate(p, axis=1).astype(BF16), vr[0, r0:r0 + width, :])
                acc = pv if acc is None else acc + pv
                l = functools.reduce(lambda a, b: a + b, p if l is None else [l] + p)
            s = _dot_nt(q, kr[0, r0:r0 + width, lo:lo + HEAD_PAD])
            s_out[:, c0:c0 + width] = s
            m_run = functools.reduce(jnp.maximum, lane_groups(s) if m_run is None else [m_run] + lane_groups(s))
        m_refs[h_scores][...] = jnp.broadcast_to(jnp.max(m_run, axis=-1, keepdims=True), (tq, LANES))
        if h_values is None:
            return None
        return acc / jnp.sum(l, axis=-1, keepdims=True)

    segment(0, 0, None)

    def body(i, carry):
        out0 = segment(i, 1, 0)
        out1 = segment(jnp.minimum(i + 1, n_tiles - 1), 0, 1)
        lane = lax.broadcasted_iota(jnp.int32, out0.shape, 1)
        o_ref[0, pl.ds(pl.multiple_of(i * tq, tq), tq), :] = jnp.where(lane < V_HEAD, out0, out1).astype(BF16)
        return carry

    lax.fori_loop(0, n_tiles, body, 0)


def _attn_call(q, k_ctx, k, v_ctx, v):
    bsz, n, hw = q.shape
    n_ctx = k_ctx.shape[1]
    tq = TQ_ATTN
    pair = 2 * HEAD_PAD
    return pl.pallas_call(
        _attn_kernel,
        grid=(bsz, MLA_HEADS // 2),
        in_specs=[
            pl.BlockSpec((1, n, pair), lambda b, h: (b, 0, h)),
            pl.BlockSpec((1, n_ctx, pair), lambda b, h: (b, 0, h)),
            pl.BlockSpec((1, n, pair), lambda b, h: (b, 0, h)),
            pl.BlockSpec((1, n_ctx, 2 * V_HEAD), lambda b, h: (b, 0, h)),
            pl.BlockSpec((1, n, 2 * V_HEAD), lambda b, h: (b, 0, h)),
        ],
        out_specs=pl.BlockSpec((1, n, 2 * V_HEAD), lambda b, h: (b, 0, h)),
        out_shape=jax.ShapeDtypeStruct((bsz, n, MLA_HEADS * V_HEAD), BF16),
        scratch_shapes=[pltpu.VMEM((tq, n_ctx + n), F32), pltpu.VMEM((tq, n_ctx + n), F32),
                        pltpu.VMEM((tq, LANES), F32), pltpu.VMEM((tq, LANES), F32)],
        compiler_params=_params(("parallel", "parallel")),
        name="attention",
    )(q, k_ctx, k, v_ctx, v)


def _ffn_tail(x1, mod_ref, g2n_ref, w1_ref, w3_ref, w2_ref, out_ref):
    h2 = _norm_mod(x1, g2n_ref[...], mod_ref[0, 3:4, :], mod_ref[0, 4:5, :]).astype(BF16)
    acc = None
    for c0, cw in FF_CHUNKS:
        a = _dot(h2, w1_ref[:, c0:c0 + cw])
        b = _dot(h2, w3_ref[:, c0:c0 + cw])
        part = _dot((_silu(a) * b).astype(BF16), w2_ref[c0:c0 + cw, :])
        acc = part if acc is None else acc + part
    out_ref[0] = x1 + mod_ref[0, 5:6, :] * acc


def _even_out_kernel(x_ref, o_ref, y_ref, mod_ref, g2n_ref, wo_ref, w1_ref, w3_ref, w2_ref, out_ref, ybuf):
    for di in range(TM_OUT // FFT_R):
        ybuf[di * FFT_R:(di + 1) * FFT_R, :] = y_ref[0, :, di, :].astype(BF16)
    hv = MLA_HEADS * V_HEAD
    mix = _dot(o_ref[0], wo_ref[0:hv, :]) + _dot(ybuf[...], wo_ref[hv:, :])
    x1 = x_ref[0] + mod_ref[0, 2:3, :] * mix
    _ffn_tail(x1, mod_ref, g2n_ref, w1_ref, w3_ref, w2_ref, out_ref)


def _even_out_call(x, o, y, mod, g2n, wo, w1, w3, w2):
    bsz, n, d = x.shape
    tm = TM_OUT
    r = FFT_R
    tok = lambda width: pl.BlockSpec((1, tm, width), lambda b, i: (b, i, 0))
    return pl.pallas_call(
        _even_out_kernel,
        grid=(bsz, n // tm),
        in_specs=[
            tok(d), tok(o.shape[-1]),
            pl.BlockSpec((1, r, tm // r, FNET_W), lambda b, i: (b, 0, i, 0)),
            pl.BlockSpec((1, 8, d), lambda b, i: (b, 0, 0)),
            _const_spec((1, d)),
            _const_spec(wo.shape), _const_spec(w1.shape), _const_spec(w3.shape), _const_spec(w2.shape),
        ],
        out_specs=tok(d),
        out_shape=jax.ShapeDtypeStruct((bsz, n, d), F32),
        scratch_shapes=[pltpu.VMEM((tm, FNET_W), BF16)],
        compiler_params=_params(("parallel", "parallel")),
        name="even_out_ffn",
    )(x, o, y, mod, g2n, wo, w1, w3, w2)


CONV_RB = 64


SUBLANES = 8


def _conf_conv_block(cbuf, dw_ref, bias, r0, ls):
    rows = CONV_RB + SUBLANES
    acc = jnp.broadcast_to(bias, (CONV_RB, LANES))
    for r in range(SUBLANES):
        part = None
        for o in range(r, CONF_WIDTH + 1, SUBLANES):
            if o == 0:
                continue
            term = dw_ref[o - 1:o, ls] * cbuf[r0 + o - r:r0 + o - r + rows, ls]
            part = term if part is None else part + term
        acc = acc + part[r:r + CONV_RB]
    return acc


def _odd_kernel(xp_ref, x_ref, xn_ref, mod_ref, g1n_ref, win_ref, dw_ref, dwb_ref, lng_ref, lnb_ref, scw_ref,
                wout_ref, g2n_ref, w1_ref, w3_ref, w2_ref, out_ref, hb_ref, cbuf, sbuf, mbuf):
    tm = x_ref.shape[1]
    i = pl.program_id(1)
    last = pl.num_programs(1) - 1
    g = g1n_ref[...]
    sh = mod_ref[0, 0:1, :]
    sc = mod_ref[0, 1:2, :]
    x = x_ref[0]
    hb_ref[0:HALO, :] = _norm_mod(xp_ref[0], g, sh, sc).astype(BF16)
    hb_ref[HALO:HALO + tm, :] = _norm_mod(x, g, sh, sc).astype(BF16)
    hb_ref[HALO + tm:, :] = _norm_mod(xn_ref[0], g, sh, sc).astype(BF16)
    u = _dot(hb_ref[...], win_ref[...])

    conf = u[:, 0:CONF_W] * jax.nn.sigmoid(u[:, CONF_W:2 * CONF_W])
    cx = u[:, 2 * CONF_W + SC_W:2 * CONF_W + 2 * SC_W] * u[:, 2 * CONF_W + 2 * SC_W:]
    keep_lo = i > 0
    keep_hi = i < last
    cbuf[0:HALO, :] = jnp.where(keep_lo, conf[0:HALO], 0.0)
    cbuf[HALO:HALO + tm, :] = conf[HALO:HALO + tm]
    cbuf[HALO + tm:, :] = jnp.where(keep_hi, conf[HALO + tm:], 0.0)
    sbuf[0:HALO, :] = jnp.where(keep_lo, cx[0:HALO], 0.0)
    sbuf[HALO:HALO + tm, :] = cx[HALO:HALO + tm]
    sbuf[HALO + tm:, :] = jnp.where(keep_hi, cx[HALO + tm:], 0.0)
    sb = u[HALO:HALO + tm, 2 * CONF_W:2 * CONF_W + SC_W]

    assert HALO == (CONF_WIDTH - 1) // 2 + 1
    pad_s = (SC_WIDTH - 1) // 2
    for rb in range(tm // CONV_RB):
        r0 = rb * CONV_RB
        for lg in range(CONF_W // LANES):
            ls = slice(lg * LANES, (lg + 1) * LANES)
            acc = _conf_conv_block(cbuf, dw_ref, dwb_ref[:, ls], r0, ls)
            mu = jnp.mean(acc, axis=-1, keepdims=True)
            cen = acc - mu
            var = jnp.mean(cen * cen, axis=-1, keepdims=True)
            yn = cen * lax.rsqrt(var + EPS) * lng_ref[:, ls] + lnb_ref[:, ls]
            mbuf[r0:r0 + CONV_RB, ls] = _silu(yn).astype(BF16)

            acc = None
            for j in range(SC_WIDTH):
                o = HALO + r0 + j - pad_s
                term = scw_ref[j:j + 1, ls] * sbuf[o:o + CONV_RB, ls]
                acc = term if acc is None else acc + term
            mbuf[r0:r0 + CONV_RB, CONF_W + lg * LANES:CONF_W + (lg + 1) * LANES] = (
                sb[r0:r0 + CONV_RB, ls] * acc).astype(BF16)

    mix = _dot(mbuf[...], wout_ref[...])
    x1 = x + mod_ref[0, 2:3, :] * mix
    _ffn_tail(x1, mod_ref, g2n_ref, w1_ref, w3_ref, w2_ref, out_ref)


def _odd_call(x, mod, g1n, win, dw, dwb, lng, lnb, scw, wout, g2n, w1, w3, w2):
    bsz, n, d = x.shape
    tm = TM_ODD
    hb = tm // HALO
    nh = n // HALO
    return pl.pallas_call(
        _odd_kernel,
        grid=(bsz, n // tm),
        in_specs=[
            pl.BlockSpec((1, HALO, d), lambda b, i: (b, jnp.maximum(i * hb - 1, 0), 0)),
            pl.BlockSpec((1, tm, d), lambda b, i: (b, i, 0)),
            pl.BlockSpec((1, HALO, d), lambda b, i: (b, jnp.minimum((i + 1) * hb, nh - 1), 0)),
            pl.BlockSpec((1, 8, d), lambda b, i: (b, 0, 0)),
            _const_spec((1, d)),
            _const_spec(win.shape),
            _const_spec(dw.shape), _const_spec(dwb.shape), _const_spec(lng.shape), _const_spec(lnb.shape),
            _const_spec(scw.shape),
            _const_spec(wout.shape),
            _const_spec((1, d)),
            _const_spec(w1.shape), _const_spec(w3.shape), _const_spec(w2.shape),
        ],
        out_specs=pl.BlockSpec((1, tm, d), lambda b, i: (b, i, 0)),
        out_shape=jax.ShapeDtypeStruct((bsz, n, d), F32),
        scratch_shapes=[
            pltpu.VMEM((tm + 2 * HALO, d), BF16),
            pltpu.VMEM((tm + 2 * HALO, CONF_W), F32),
            pltpu.VMEM((tm + 2 * HALO, SC_W), F32),
            pltpu.VMEM((tm, CONF_W + SC_W), BF16),
        ],
        compiler_params=_params(("parallel", "arbitrary")),
        name="odd_layer",
    )(x, x, x, mod, g1n, win, dw, dwb, lng, lnb, scw, wout, g2n, w1, w3, w2)


def _dft_tables():
    r = FFT_R
    gw = FNET_GROUP_W
    idx = np.arange(gw)
    ang = 2.0 * np.pi * ((idx[:, None] * idx[None, :]) % gw) / gw
    cs = np.concatenate([np.cos(ang), -np.sin(ang)], axis=1) / np.sqrt(gw)
    a = np.arange(r)
    ang1 = 2.0 * np.pi * ((a[:, None] * a[None, :]) % r) / r
    fr, fi = np.cos(ang1) / 8.0, -np.sin(ang1) / 8.0
    f1 = np.block([[fr, -fi], [fi, fr]])
    n = r * r
    c = a[:, None, None]
    dd = a[None, :, None]
    bb = a[None, None, :]
    ang2 = 2.0 * np.pi * ((bb * (c + r * dd)) % n) / n
    g2 = np.concatenate([np.cos(ang2), np.sin(ang2)], axis=2) / 8.0
    return {name: jnp.asarray(t, F32).astype(BF16) for name, t in (("cs", cs), ("f1", f1), ("g2", g2))}


def _rope_tables(n):
    rows = n // GRID_W
    row = jnp.broadcast_to(jnp.arange(rows, dtype=F32)[:, None], (rows, GRID_W)).reshape(n)
    col = jnp.broadcast_to(jnp.arange(GRID_W, dtype=F32)[None, :], (rows, GRID_W)).reshape(n)
    per_axis = QK_ROPE // 4
    inv_freq = ROPE_BASE ** (-jnp.arange(per_axis, dtype=F32) / per_axis)
    ang = jnp.concatenate([row[:, None] * inv_freq, col[:, None] * inv_freq], axis=-1)
    cos, sin = jnp.cos(ang), jnp.sin(ang)
    one = jnp.ones((n, QK_NOPE), F32)
    zn = jnp.zeros((n, QK_NOPE), F32)
    zh = jnp.zeros((n, ROPE_HALF), F32)
    tail = LANES - QK_HEAD
    rc = jnp.concatenate([one, cos, cos, jnp.ones((n, tail), F32)], axis=-1)
    rsa = jnp.concatenate([zn, zh, sin, jnp.zeros((n, tail), F32)], axis=-1)
    rsb = jnp.concatenate([zn, -sin, zh, jnp.zeros((n, tail), F32)], axis=-1)
    return rc, rsa, rsb


def _pad_heads(w, width):
    r = w.shape[0]
    return jnp.pad(w, ((0, 0), (0, 0), (0, HEAD_PAD - width))).reshape(r, MLA_HEADS * HEAD_PAD)


def _pad_gain(g):
    return jnp.pad(g, (0, HEAD_PAD - QK_HEAD)).reshape(1, HEAD_PAD)


def kernel(x, c, ctx, c_ctx, ada_w, ada_b, norm1_g, norm2_g, ffn_w1, ffn_w3, ffn_w2, a_w_in, a_q_ln_g, a_kv_ln_g, a_w_uq, a_w_uk, a_w_uv, a_q_norm_g, a_k_norm_g, a_w_out, b_w_in, b_conf_dw, b_conf_dw_b, b_conf_ln_g, b_conf_ln_b, b_sc_dw, b_w_out):
    bsz, n, d = x.shape
    depth = ada_w.shape[0]
    assert depth == 2 and d == D_MODEL and n == FFT_R * FFT_R and ctx.shape[1] == CTX_LEN

    rows = 16
    cvec = jnp.concatenate([c, c_ctx[None, :], jnp.zeros((rows - bsz - 1, d), F32)], axis=0)
    ada = _ada_call(cvec, ada_w, ada_b).reshape(depth, rows, 6, d)
    unit = jnp.array([0.0, 1.0, 0.0, 0.0, 1.0, 0.0], F32)[None, None, :, None]
    mods = jnp.pad(ada + unit, ((0, 0), (0, 0), (0, 2), (0, 0)))

    tables = _dft_tables()
    tables["rc"], tables["rsa"], tables["rsb"] = _rope_tables(n)

    w_in = a_w_in[0]
    kr_cols = jnp.pad(w_in[:, Q_LORA + KV_LORA:EVEN_KV_END], ((0, 0), (QK_NOPE, LANES - QK_HEAD)))
    w = {
        "wq": jnp.concatenate([w_in[:, :Q_LORA], kr_cols], axis=1).astype(BF16),
        "wkv": w_in[:, Q_LORA:Q_LORA + KV_LORA].astype(BF16),
        "wf": w_in[:, EVEN_KV_END:].astype(BF16),
        "qlg": a_q_ln_g[0].reshape(1, Q_LORA),
        "kvlg": a_kv_ln_g[0].reshape(1, KV_LORA),
        "wuq": _pad_heads(a_w_uq[0], QK_HEAD).astype(BF16),
        "wuk": _pad_heads(a_w_uk[0], QK_NOPE).astype(BF16),
        "wuv": a_w_uv[0].reshape(KV_LORA, MLA_HEADS * V_HEAD).astype(BF16),
        "gq": _pad_gain(a_q_norm_g[0]),
        "gk": _pad_gain(a_k_norm_g[0]),
    }
    g1 = norm1_g[0].reshape(1, d)
    q, k, v, z = _even_in_call(x, mods[0, :bsz], g1, w, tables)
    k_ctx, v_ctx = _ctx_in_call(ctx, mods[0, bsz:bsz + 1], g1, w)
    y = _seq_dft_call(z, tables)
    o = _attn_call(q, k_ctx, k, v_ctx, v)
    x = _even_out_call(x, o, y, mods[0, :bsz], norm2_g[0].reshape(1, d), a_w_out[0].astype(BF16),
                       ffn_w1[0].astype(BF16), ffn_w3[0].astype(BF16), ffn_w2[0].astype(BF16))

    x = _odd_call(x, mods[1, :bsz], norm1_g[1].reshape(1, d), b_w_in[0].astype(BF16),
                  b_conf_dw[0], b_conf_dw_b[0].reshape(1, CONF_W), b_conf_ln_g[0].reshape(1, CONF_W),
                  b_conf_ln_b[0].reshape(1, CONF_W), b_sc_dw[0], b_w_out[0].astype(BF16),
                  norm2_g[1].reshape(1, d), ffn_w1[1].astype(BF16), ffn_w3[1].astype(BF16),
                  ffn_w2[1].astype(BF16))
    return x
```

```python
import functools
import math

import numpy as np
import jax
import jax.numpy as jnp
from jax import lax
from jax.experimental import pallas as pl
from jax.experimental.pallas import tpu as pltpu

F32 = jnp.float32
BF16 = jnp.bfloat16

D_MODEL = 1024
CTX_LEN = 256
GRID_W = 64
EPS = 1e-6
MLA_HEADS = 8
QK_NOPE = 64
QK_ROPE = 32
QK_HEAD = QK_NOPE + QK_ROPE
V_HEAD = 64
Q_LORA = 384
KV_LORA = 256
ROPE_BASE = 10000.0
FNET_GROUPS = 4
FNET_GROUP_W = 128
FNET_W = FNET_GROUPS * FNET_GROUP_W
CONF_GROUPS = 4
CONF_W = 512
CONF_WIDTH = 31
SC_W = 512
SC_WIDTH = 3
D_FF = 2816
EVEN_KV_END = Q_LORA + KV_LORA + QK_ROPE

LANES = 128
HEAD_PAD = LANES
ROPE_HALF = QK_ROPE // 2
FFT_R = 64
HALO = 16
VMEM_LIMIT = 56 * 1024 * 1024

TM_EVEN_IN = 512
TQ_ATTN = 512
KC_ATTN = 512
TM_OUT = 512
TM_ODD = 512
FF_CHUNKS = tuple((c, min(512, D_FF - c)) for c in range(0, D_FF, 512))

Q_SCALE = QK_HEAD ** -0.5 * math.log2(math.e)


def _dot(a, b):
    return jnp.dot(a, b, preferred_element_type=F32)


def _dot_nt(a, b):
    return lax.dot_general(a, b, (((1,), (1,)), ((), ())), preferred_element_type=F32)


def _inv_rms(x, n):
    return lax.rsqrt(jnp.sum(x * x, axis=-1, keepdims=True) * (1.0 / n) + EPS)


def _norm_mod(x, g, shift, scale1p):
    return (x * _inv_rms(x, x.shape[-1]) * g) * scale1p + shift


def _silu(a):
    return a * jax.nn.sigmoid(a)


def _const_spec(shape):
    nd = len(shape)
    return pl.BlockSpec(shape, lambda *_: (0,) * nd, pipeline_mode=pl.Buffered(1))


def _params(sem, flags=None):
    return pltpu.CompilerParams(dimension_semantics=sem, vmem_limit_bytes=VMEM_LIMIT, flags=flags)


def _ada_kernel(c_ref, w_ref, b_ref, o_ref):
    s = _silu(c_ref[...]).astype(BF16)
    o_ref[0] = _dot(s, w_ref[0].astype(BF16)) + b_ref[0]


def _ada_call(cvec, ada_w, ada_b):
    depth, d, n6 = ada_w.shape
    rows = cvec.shape[0]
    tn = 1536
    return pl.pallas_call(
        _ada_kernel,
        grid=(depth, n6 // tn),
        in_specs=[
            pl.BlockSpec((rows, d), lambda l, j: (0, 0)),
            pl.BlockSpec((1, d, tn), lambda l, j: (l, 0, j)),
            pl.BlockSpec((1, 1, tn), lambda l, j: (l, 0, j)),
        ],
        out_specs=pl.BlockSpec((1, rows, tn), lambda l, j: (l, 0, j)),
        out_shape=jax.ShapeDtypeStruct((depth, rows, n6), F32),
        compiler_params=_params(("arbitrary", "arbitrary")),
        name="ada_mod",
    )(cvec, ada_w, ada_b.reshape(depth, 1, n6))


def _head_inv_rms(t, ones):
    ss = _dot((t * t).astype(BF16), ones)
    return lax.rsqrt(ss * (1.0 / QK_HEAD) + EPS)


def _keys_values(hb, krb, rope, wkv_ref, kvlg_ref, wuk_ref, wuv_ref, gk_ref, ones_ref, k_ref, v_ref):
    ckv = _dot(hb, wkv_ref[...])
    ckvn = (ckv * _inv_rms(ckv, KV_LORA) * kvlg_ref[...]).astype(BF16)
    kf = _dot(ckvn, wuk_ref[...])
    v_ref[0] = _dot(ckvn, wuv_ref[...]).astype(BF16)
    gk = gk_ref[...]
    krg = krb * gk
    if rope is not None:
        rc, rs = rope
        lane = lax.broadcasted_iota(jnp.int32, krg.shape, 1)
        partner = jnp.where(lane < QK_NOPE + ROPE_HALF, pltpu.roll(krg, LANES - ROPE_HALF, 1),
                            pltpu.roll(krg, ROPE_HALF, 1))
        krg = krg * rc + partner * rs
    ones = ones_ref[...]
    for h in range(MLA_HEADS):
        kn = kf[:, h * HEAD_PAD:(h + 1) * HEAD_PAD]
        r = _head_inv_rms(kn + krb, ones)
        k_ref[0, :, h * HEAD_PAD:(h + 1) * HEAD_PAD] = ((kn * gk + krg) * r).astype(BF16)


def _even_in_kernel(x_ref, mod_ref, g_ref, wq_ref, wkv_ref, wf_ref, qlg_ref, kvlg_ref, wuq_ref, wuqs_ref, wuk_ref,
                    wuv_ref, gq_ref, gqs_ref, gk_ref, rc_ref, rs_ref, cs_ref, ones_ref,
                    q_ref, k_ref, v_ref, z_ref):
    hb = _norm_mod(x_ref[0], g_ref[...], mod_ref[0, 0:1, :], mod_ref[0, 1:2, :]).astype(BF16)
    rc, rs = rc_ref[...], rs_ref[...]
    uq = _dot(hb, wq_ref[...])
    _keys_values(hb, uq[:, Q_LORA:], (rc, rs), wkv_ref, kvlg_ref, wuk_ref, wuv_ref, gk_ref, ones_ref,
                 k_ref, v_ref)

    cq = uq[:, :Q_LORA]
    cqn = (cq * _inv_rms(cq, Q_LORA) * qlg_ref[...]).astype(BF16)
    qf = _dot(cqn, wuq_ref[...])
    qs = _dot(cqn, wuqs_ref[...])
    gq = gq_ref[...] * Q_SCALE
    gqs = gqs_ref[...] * Q_SCALE
    ones = ones_ref[...]
    for h in range(MLA_HEADS):
        sl = slice(h * HEAD_PAD, (h + 1) * HEAD_PAD)
        qh = qf[:, sl]
        r = _head_inv_rms(qh, ones)
        q_ref[0, :, sl] = ((qh * gq * rc + qs[:, sl] * gqs * rs) * r).astype(BF16)

    uf = _dot(hb, wf_ref[...]).astype(BF16)
    for g in range(FNET_GROUPS):
        zz = _dot(uf[:, g * LANES:(g + 1) * LANES], cs_ref[...]).astype(BF16)
        z_ref[0, :, g * LANES:(g + 1) * LANES] = zz[:, :LANES]
        z_ref[0, :, FNET_W + g * LANES:FNET_W + (g + 1) * LANES] = zz[:, LANES:]


def _ctx_in_kernel(x_ref, mod_ref, g_ref, wq_ref, wkv_ref, kvlg_ref, wuk_ref, wuv_ref, gk_ref, ones_ref,
                   k_ref, v_ref):
    hb = _norm_mod(x_ref[0], g_ref[...], mod_ref[0, 0:1, :], mod_ref[0, 1:2, :]).astype(BF16)
    krb = _dot(hb, wq_ref[:, Q_LORA:])
    _keys_values(hb, krb, None, wkv_ref, kvlg_ref, wuk_ref, wuv_ref, gk_ref, ones_ref, k_ref, v_ref)


def _even_in_call(x, mod, g, w, tables):
    bsz, n, d = x.shape
    tm = TM_EVEN_IN
    hw = MLA_HEADS * HEAD_PAD
    tok = lambda width: pl.BlockSpec((1, tm, width), lambda b, i: (b, i, 0))
    tab = pl.BlockSpec((tm, LANES), lambda b, i: (i, 0))
    return pl.pallas_call(
        _even_in_kernel,
        grid=(bsz, n // tm),
        in_specs=[
            tok(d),
            pl.BlockSpec((1, 8, d), lambda b, i: (b, 0, 0)),
            _const_spec((1, d)),
            _const_spec(w["wq"].shape), _const_spec(w["wkv"].shape), _const_spec(w["wf"].shape),
            _const_spec((1, Q_LORA)), _const_spec((1, KV_LORA)),
            _const_spec(w["wuq"].shape), _const_spec(w["wuqs"].shape), _const_spec(w["wuk"].shape),
            _const_spec(w["wuv"].shape),
            _const_spec((1, LANES)), _const_spec((1, LANES)), _const_spec((1, LANES)),
            tab, tab,
            _const_spec(tables["cs"].shape), _const_spec(tables["ones"].shape),
        ],
        out_specs=[tok(hw), tok(hw), tok(MLA_HEADS * V_HEAD), tok(2 * FNET_W)],
        out_shape=[
            jax.ShapeDtypeStruct((bsz, n, hw), BF16),
            jax.ShapeDtypeStruct((bsz, n, hw), BF16),
            jax.ShapeDtypeStruct((bsz, n, MLA_HEADS * V_HEAD), BF16),
            jax.ShapeDtypeStruct((bsz, n, 2 * FNET_W), BF16),
        ],
        compiler_params=_params(("parallel", "parallel")),
        name="even_in",
    )(x, mod, g, w["wq"], w["wkv"], w["wf"], w["qlg"], w["kvlg"], w["wuq"], w["wuqs"], w["wuk"], w["wuv"],
      w["gq"], w["gqs"], w["gk"], tables["rc"], tables["rs"], tables["cs"], tables["ones"])


def _ctx_in_call(ctx, mod, g, w, tables):
    bsz, n, d = ctx.shape
    hw = MLA_HEADS * HEAD_PAD
    tok = lambda width: pl.BlockSpec((1, n, width), lambda b: (b, 0, 0))
    return pl.pallas_call(
        _ctx_in_kernel,
        grid=(bsz,),
        in_specs=[
            tok(d),
            pl.BlockSpec((1, 8, d), lambda b: (0, 0, 0)),
            _const_spec((1, d)),
            _const_spec(w["wq"].shape), _const_spec(w["wkv"].shape),
            _const_spec((1, KV_LORA)),
            _const_spec(w["wuk"].shape), _const_spec(w["wuv"].shape),
            _const_spec((1, LANES)), _const_spec(tables["ones"].shape),
        ],
        out_specs=[tok(hw), tok(MLA_HEADS * V_HEAD)],
        out_shape=[
            jax.ShapeDtypeStruct((bsz, n, hw), BF16),
            jax.ShapeDtypeStruct((bsz, n, MLA_HEADS * V_HEAD), BF16),
        ],
        compiler_params=_params(("parallel",)),
        name="ctx_in",
    )(ctx, mod, g, w["wq"], w["wkv"], w["kvlg"], w["wuk"], w["wuv"], w["gk"], tables["ones"])


DFT_NB = 8


def _dft1_kernel(z_ref, f_ref, t_ref):
    w2 = 2 * FNET_W
    for bi in range(DFT_NB):
        z = z_ref[0, :, bi * w2:(bi + 1) * w2]
        x = jnp.concatenate([z[:, :FNET_W], z[:, FNET_W:]], axis=0)
        t_ref[0, :, bi * FNET_W:(bi + 1) * FNET_W] = _dot(f_ref[...], x).astype(BF16)


def _dft2_kernel(tr_ref, ti_ref, g_ref, y_ref):
    for ci in range(DFT_NB):
        x = jnp.concatenate([tr_ref[0, ci], ti_ref[0, ci]], axis=0)
        y_ref[0, :, ci * FNET_W:(ci + 1) * FNET_W] = _dot(g_ref[ci], x).astype(BF16)


def _seq_dft_call(z, tables):
    bsz, n, w2 = z.shape
    r = FFT_R
    nb = DFT_NB
    t = pl.pallas_call(
        _dft1_kernel,
        grid=(bsz, r // nb),
        in_specs=[
            pl.BlockSpec((1, r, nb * w2), lambda b, j: (b, 0, j)),
            _const_spec((2 * r, 2 * r)),
        ],
        out_specs=pl.BlockSpec((1, 2 * r, nb * FNET_W), lambda b, j: (b, 0, j)),
        out_shape=jax.ShapeDtypeStruct((bsz, 2 * r, r * FNET_W), BF16),
        compiler_params=_params(("parallel", "parallel")),
        name="seq_dft1",
    )(z.reshape(bsz, r, r * w2), tables["f1"])
    t4 = t.reshape(bsz, 2 * r, r, FNET_W)
    y = pl.pallas_call(
        _dft2_kernel,
        grid=(r // nb, bsz),
        in_specs=[
            pl.BlockSpec((1, nb, r, FNET_W), lambda j, b: (b, j, 0, 0)),
            pl.BlockSpec((1, nb, r, FNET_W), lambda j, b: (b, r // nb + j, 0, 0)),
            pl.BlockSpec((nb, r, 2 * r), lambda j, b: (j, 0, 0)),
        ],
        out_specs=pl.BlockSpec((1, r, nb * FNET_W), lambda j, b: (b, 0, j)),
        out_shape=jax.ShapeDtypeStruct((bsz, r, r * FNET_W), BF16),
        compiler_params=_params(("parallel", "parallel")),
        name="seq_dft2",
    )(t4, t4, tables["g2"])
    return y.reshape(bsz, n, FNET_W)


def _attn_kernel(q_ref, kc_ref, k_ref, vc_ref, v_ref, o_ref, s0_ref, s1_ref, m0_ref, m1_ref):
    n_ctx = kc_ref.shape[1]
    n_chunks = k_ref.shape[1] // KC_ATTN
    tq = TQ_ATTN
    n_tiles = q_ref.shape[1] // tq
    s_refs = (s0_ref, s1_ref)
    m_refs = (m0_ref, m1_ref)
    chunks = [(kc_ref, vc_ref, 0, n_ctx, 0)] + [
        (k_ref, v_ref, j * KC_ATTN, KC_ATTN, n_ctx + j * KC_ATTN) for j in range(n_chunks)]

    def lane_groups(x):
        return [x[:, t * LANES:(t + 1) * LANES] for t in range(x.shape[1] // LANES)]

    def segment(i, h_scores, h_values):
        lo = h_scores * HEAD_PAD
        q = q_ref[0, pl.ds(pl.multiple_of(i * tq, tq), tq), lo:lo + HEAD_PAD]
        s_out = s_refs[h_scores]
        m_run = acc = l = None
        if h_values is not None:
            s_in = s_refs[h_values]
            m_in = m_refs[h_values][...]
        for kr, vr, r0, width, c0 in chunks:
            if h_values is not None:
                p = [jnp.exp2(g - m_in) for g in lane_groups(s_in[:, c0:c0 + width])]
                pv = _dot(jnp.concatenate(p, axis=1).astype(BF16), vr[0, r0:r0 + width, :])
                acc = pv if acc is None else acc + pv
                l = functools.reduce(lambda a, b: a + b, p if l is None else [l] + p)
            s = _dot_nt(q, kr[0, r0:r0 + width, lo:lo + HEAD_PAD])
            s_out[:, c0:c0 + width] = s
            m_run = functools.reduce(jnp.maximum, lane_groups(s) if m_run is None else [m_run] + lane_groups(s))
        m_refs[h_scores][...] = jnp.broadcast_to(jnp.max(m_run, axis=-1, keepdims=True), (tq, LANES))
        if h_values is None:
            return None
        return acc / jnp.sum(l, axis=-1, keepdims=True)

    segment(0, 0, None)

    def body(i, carry):
        out0 = segment(i, 1, 0)
        out1 = segment(jnp.minimum(i + 1, n_tiles - 1), 0, 1)
        lane = lax.broadcasted_iota(jnp.int32, out0.shape, 1)
        o_ref[0, pl.ds(pl.multiple_of(i * tq, tq), tq), :] = jnp.where(lane < V_HEAD, out0, out1).astype(BF16)
        return carry

    lax.fori_loop(0, n_tiles, body, 0)


def _attn_call(q, k_ctx, k, v_ctx, v):
    bsz, n, hw = q.shape
    n_ctx = k_ctx.shape[1]
    tq = TQ_ATTN
    pair = 2 * HEAD_PAD
    return pl.pallas_call(
        _attn_kernel,
        grid=(bsz, MLA_HEADS // 2),
        in_specs=[
            pl.BlockSpec((1, n, pair), lambda b, h: (b, 0, h)),
            pl.BlockSpec((1, n_ctx, pair), lambda b, h: (b, 0, h)),
            pl.BlockSpec((1, n, pair), lambda b, h: (b, 0, h)),
            pl.BlockSpec((1, n_ctx, 2 * V_HEAD), lambda b, h: (b, 0, h)),
            pl.BlockSpec((1, n, 2 * V_HEAD), lambda b, h: (b, 0, h)),
        ],
        out_specs=pl.BlockSpec((1, n, 2 * V_HEAD), lambda b, h: (b, 0, h)),
        out_shape=jax.ShapeDtypeStruct((bsz, n, MLA_HEADS * V_HEAD), BF16),
        scratch_shapes=[pltpu.VMEM((tq, n_ctx + n), F32), pltpu.VMEM((tq, n_ctx + n), F32),
                        pltpu.VMEM((tq, LANES), F32), pltpu.VMEM((tq, LANES), F32)],
        compiler_params=_params(("parallel", "parallel")),
        name="attention",
    )(q, k_ctx, k, v_ctx, v)


def _ffn_tail(x1, mod_ref, g2n_ref, w1_ref, w3_ref, w2_ref, out_ref):
    h2 = _norm_mod(x1, g2n_ref[...], mod_ref[0, 3:4, :], mod_ref[0, 4:5, :]).astype(BF16)
    acc = None
    for c0, cw in FF_CHUNKS:
        a = _dot(h2, w1_ref[:, c0:c0 + cw])
        b = _dot(h2, w3_ref[:, c0:c0 + cw])
        part = _dot((_silu(a) * b).astype(BF16), w2_ref[c0:c0 + cw, :])
        acc = part if acc is None else acc + part
    out_ref[0] = x1 + mod_ref[0, 5:6, :] * acc


def _even_out_kernel(x_ref, o_ref, y_ref, mod_ref, g2n_ref, wo_ref, w1_ref, w3_ref, w2_ref, out_ref):
    hv = MLA_HEADS * V_HEAD
    mix = _dot(o_ref[0], wo_ref[0:hv, :]) + _dot(y_ref[0], wo_ref[hv:, :])
    x1 = x_ref[0] + mod_ref[0, 2:3, :] * mix
    _ffn_tail(x1, mod_ref, g2n_ref, w1_ref, w3_ref, w2_ref, out_ref)


def _even_out_call(x, o, y, mod, g2n, wo, w1, w3, w2):
    bsz, n, d = x.shape
    tm = TM_OUT
    tok = lambda width: pl.BlockSpec((1, tm, width), lambda b, i: (b, i, 0))
    return pl.pallas_call(
        _even_out_kernel,
        grid=(bsz, n // tm),
        in_specs=[
            tok(d), tok(o.shape[-1]), tok(y.shape[-1]),
            pl.BlockSpec((1, 8, d), lambda b, i: (b, 0, 0)),
            _const_spec((1, d)),
            _const_spec(wo.shape), _const_spec(w1.shape), _const_spec(w3.shape), _const_spec(w2.shape),
        ],
        out_specs=tok(d),
        out_shape=jax.ShapeDtypeStruct((bsz, n, d), F32),
        compiler_params=_params(("parallel", "parallel")),
        name="even_out_ffn",
    )(x, o, y, mod, g2n, wo, w1, w3, w2)


CONV_RB = 64


SUBLANES = 8


def _conf_conv_block(cbuf, dw_ref, bias, r0, ls):
    rows = CONV_RB + SUBLANES
    acc = jnp.broadcast_to(bias, (CONV_RB, LANES))
    for r in range(SUBLANES):
        part = None
        for o in range(r, CONF_WIDTH + 1, SUBLANES):
            if o == 0:
                continue
            term = dw_ref[o - 1:o, ls] * cbuf[r0 + o - r:r0 + o - r + rows, ls]
            part = term if part is None else part + term
        acc = acc + part[r:r + CONV_RB]
    return acc


def _odd_kernel(xp_ref, x_ref, xn_ref, mod_ref, g1n_ref, win_ref, dw_ref, dwb_ref, lng_ref, lnb_ref, scw_ref,
                wout_ref, g2n_ref, w1_ref, w3_ref, w2_ref, out_ref, hb_ref, cbuf, sbuf, mbuf):
    tm = x_ref.shape[1]
    i = pl.program_id(1)
    last = pl.num_programs(1) - 1
    g = g1n_ref[...]
    sh = mod_ref[0, 0:1, :]
    sc = mod_ref[0, 1:2, :]
    x = x_ref[0]
    hb_ref[0:HALO, :] = _norm_mod(xp_ref[0], g, sh, sc).astype(BF16)
    hb_ref[HALO:HALO + tm, :] = _norm_mod(x, g, sh, sc).astype(BF16)
    hb_ref[HALO + tm:, :] = _norm_mod(xn_ref[0], g, sh, sc).astype(BF16)
    u = _dot(hb_ref[...], win_ref[...])

    conf = u[:, 0:CONF_W] * jax.nn.sigmoid(u[:, CONF_W:2 * CONF_W])
    cx = u[:, 2 * CONF_W + SC_W:2 * CONF_W + 2 * SC_W] * u[:, 2 * CONF_W + 2 * SC_W:]
    keep_lo = i > 0
    keep_hi = i < last
    cbuf[0:HALO, :] = jnp.where(keep_lo, conf[0:HALO], 0.0)
    cbuf[HALO:HALO + tm, :] = conf[HALO:HALO + tm]
    cbuf[HALO + tm:, :] = jnp.where(keep_hi, conf[HALO + tm:], 0.0)
    sbuf[0:HALO, :] = jnp.where(keep_lo, cx[0:HALO], 0.0)
    sbuf[HALO:HALO + tm, :] = cx[HALO:HALO + tm]
    sbuf[HALO + tm:, :] = jnp.where(keep_hi, cx[HALO + tm:], 0.0)
    sb = u[HALO:HALO + tm, 2 * CONF_W:2 * CONF_W + SC_W]

    assert HALO == (CONF_WIDTH - 1) // 2 + 1
    pad_s = (SC_WIDTH - 1) // 2
    for rb in range(tm // CONV_RB):
        r0 = rb * CONV_RB
        for lg in range(CONF_W // LANES):
            ls = slice(lg * LANES, (lg + 1) * LANES)
            acc = _conf_conv_block(cbuf, dw_ref, dwb_ref[:, ls], r0, ls)
            mu = jnp.mean(acc, axis=-1, keepdims=True)
            cen = acc - mu
            var = jnp.mean(cen * cen, axis=-1, keepdims=True)
            yn = cen * lax.rsqrt(var + EPS) * lng_ref[:, ls] + lnb_ref[:, ls]
            mbuf[r0:r0 + CONV_RB, ls] = _silu(yn).astype(BF16)

            acc = None
            for j in range(SC_WIDTH):
                o = HALO + r0 + j - pad_s
                term = scw_ref[j:j + 1, ls] * sbuf[o:o + CONV_RB, ls]
                acc = term if acc is None else acc + term
            mbuf[r0:r0 + CONV_RB, CONF_W + lg * LANES:CONF_W + (lg + 1) * LANES] = (
                sb[r0:r0 + CONV_RB, ls] * acc).astype(BF16)

    mix = _dot(mbuf[...], wout_ref[...])
    x1 = x + mod_ref[0, 2:3, :] * mix
    _ffn_tail(x1, mod_ref, g2n_ref, w1_ref, w3_ref, w2_ref, out_ref)


def _odd_call(x, mod, g1n, win, dw, dwb, lng, lnb, scw, wout, g2n, w1, w3, w2):
    bsz, n, d = x.shape
    tm = TM_ODD
    hb = tm // HALO
    nh = n // HALO
    return pl.pallas_call(
        _odd_kernel,
        grid=(bsz, n // tm),
        in_specs=[
            pl.BlockSpec((1, HALO, d), lambda b, i: (b, jnp.maximum(i * hb - 1, 0), 0)),
            pl.BlockSpec((1, tm, d), lambda b, i: (b, i, 0)),
            pl.BlockSpec((1, HALO, d), lambda b, i: (b, jnp.minimum((i + 1) * hb, nh - 1), 0)),
            pl.BlockSpec((1, 8, d), lambda b, i: (b, 0, 0)),
            _const_spec((1, d)),
            _const_spec(win.shape),
            _const_spec(dw.shape), _const_spec(dwb.shape), _const_spec(lng.shape), _const_spec(lnb.shape),
            _const_spec(scw.shape),
            _const_spec(wout.shape),
            _const_spec((1, d)),
            _const_spec(w1.shape), _const_spec(w3.shape), _const_spec(w2.shape),
        ],
        out_specs=pl.BlockSpec((1, tm, d), lambda b, i: (b, i, 0)),
        out_shape=jax.ShapeDtypeStruct((bsz, n, d), F32),
        scratch_shapes=[
            pltpu.VMEM((tm + 2 * HALO, d), BF16),
            pltpu.VMEM((tm + 2 * HALO, CONF_W), F32),
            pltpu.VMEM((tm + 2 * HALO, SC_W), F32),
            pltpu.VMEM((tm, CONF_W + SC_W), BF16),
        ],
        compiler_params=_params(("parallel", "arbitrary")),
        name="odd_layer",
    )(x, x, x, mod, g1n, win, dw, dwb, lng, lnb, scw, wout, g2n, w1, w3, w2)


def _dft_tables():
    r = FFT_R
    gw = FNET_GROUP_W
    idx = np.arange(gw)
    ang = 2.0 * np.pi * ((idx[:, None] * idx[None, :]) % gw) / gw
    cs = np.concatenate([np.cos(ang), -np.sin(ang)], axis=1) / np.sqrt(gw)
    a = np.arange(r)
    ang1 = 2.0 * np.pi * ((a[:, None] * a[None, :]) % r) / r
    fr, fi = np.cos(ang1) / 8.0, -np.sin(ang1) / 8.0
    f1 = np.block([[fr, -fi], [fi, fr]])
    n = r * r
    c = a[:, None, None]
    dd = a[None, :, None]
    bb = a[None, None, :]
    ang2 = 2.0 * np.pi * ((bb * (c + r * dd)) % n) / n
    g2 = np.concatenate([np.cos(ang2), np.sin(ang2)], axis=2) / 8.0
    return {name: jnp.asarray(t, F32).astype(BF16) for name, t in (("cs", cs), ("f1", f1), ("g2", g2))}


def _rope_tables(n):
    rows = n // GRID_W
    row = jnp.broadcast_to(jnp.arange(rows, dtype=F32)[:, None], (rows, GRID_W)).reshape(n)
    col = jnp.broadcast_to(jnp.arange(GRID_W, dtype=F32)[None, :], (rows, GRID_W)).reshape(n)
    per_axis = QK_ROPE // 4
    inv_freq = ROPE_BASE ** (-jnp.arange(per_axis, dtype=F32) / per_axis)
    ang = jnp.concatenate([row[:, None] * inv_freq, col[:, None] * inv_freq], axis=-1)
    cos, sin = jnp.cos(ang), jnp.sin(ang)
    tail = LANES - QK_HEAD
    rc = jnp.concatenate([jnp.ones((n, QK_NOPE), F32), cos, cos, jnp.ones((n, tail), F32)], axis=-1)
    rs = jnp.concatenate([jnp.zeros((n, QK_NOPE), F32), -sin, sin, jnp.zeros((n, tail), F32)], axis=-1)
    return rc, rs


def _swap_rope_halves(t):
    r1 = t[..., QK_NOPE:QK_NOPE + ROPE_HALF]
    r2 = t[..., QK_NOPE + ROPE_HALF:]
    return jnp.concatenate([jnp.zeros_like(t[..., :QK_NOPE]), r2, r1], axis=-1)


def _pad_heads(w, width):
    r = w.shape[0]
    return jnp.pad(w, ((0, 0), (0, 0), (0, HEAD_PAD - width))).reshape(r, MLA_HEADS * HEAD_PAD)


def _pad_gain(g):
    return jnp.pad(g, (0, HEAD_PAD - QK_HEAD)).reshape(1, HEAD_PAD)


def kernel(x, c, ctx, c_ctx, ada_w, ada_b, norm1_g, norm2_g, ffn_w1, ffn_w3, ffn_w2, a_w_in, a_q_ln_g, a_kv_ln_g, a_w_uq, a_w_uk, a_w_uv, a_q_norm_g, a_k_norm_g, a_w_out, b_w_in, b_conf_dw, b_conf_dw_b, b_conf_ln_g, b_conf_ln_b, b_sc_dw, b_w_out):
    bsz, n, d = x.shape
    depth = ada_w.shape[0]
    assert depth == 2 and d == D_MODEL and n == FFT_R * FFT_R and ctx.shape[1] == CTX_LEN

    rows = 16
    cvec = jnp.concatenate([c, c_ctx[None, :], jnp.zeros((rows - bsz - 1, d), F32)], axis=0)
    ada = _ada_call(cvec, ada_w, ada_b).reshape(depth, rows, 6, d)
    unit = jnp.array([0.0, 1.0, 0.0, 0.0, 1.0, 0.0], F32)[None, None, :, None]
    mods = jnp.pad(ada + unit, ((0, 0), (0, 0), (0, 2), (0, 0)))

    tables = _dft_tables()
    tables["rc"], tables["rs"] = _rope_tables(n)
    tables["ones"] = jnp.ones((LANES, LANES), BF16)

    w_in = a_w_in[0]
    kr_cols = jnp.pad(w_in[:, Q_LORA + KV_LORA:EVEN_KV_END], ((0, 0), (QK_NOPE, LANES - QK_HEAD)))
    w = {
        "wq": jnp.concatenate([w_in[:, :Q_LORA], kr_cols], axis=1).astype(BF16),
        "wkv": w_in[:, Q_LORA:Q_LORA + KV_LORA].astype(BF16),
        "wf": w_in[:, EVEN_KV_END:].astype(BF16),
        "qlg": a_q_ln_g[0].reshape(1, Q_LORA),
        "kvlg": a_kv_ln_g[0].reshape(1, KV_LORA),
        "wuq": _pad_heads(a_w_uq[0], QK_HEAD).astype(BF16),
        "wuqs": _pad_heads(_swap_rope_halves(a_w_uq[0]), QK_HEAD).astype(BF16),
        "wuk": _pad_heads(a_w_uk[0], QK_NOPE).astype(BF16),
        "wuv": a_w_uv[0].reshape(KV_LORA, MLA_HEADS * V_HEAD).astype(BF16),
        "gq": _pad_gain(a_q_norm_g[0]),
        "gqs": _pad_gain(_swap_rope_halves(a_q_norm_g[0])),
        "gk": _pad_gain(a_k_norm_g[0]),
    }
    g1 = norm1_g[0].reshape(1, d)
    q, k, v, z = _even_in_call(x, mods[0, :bsz], g1, w, tables)
    k_ctx, v_ctx = _ctx_in_call(ctx, mods[0, bsz:bsz + 1], g1, w, tables)
    y = _seq_dft_call(z, tables)
    o = _attn_call(q, k_ctx, k, v_ctx, v)
    x = _even_out_call(x, o, y, mods[0, :bsz], norm2_g[0].reshape(1, d), a_w_out[0].astype(BF16),
                       ffn_w1[0].astype(BF16), ffn_w3[0].astype(BF16), ffn_w2[0].astype(BF16))

    x = _odd_call(x, mods[1, :bsz], norm1_g[1].reshape(1, d), b_w_in[0].astype(BF16),
                  b_conf_dw[0], b_conf_dw_b[0].reshape(1, CONF_W), b_conf_ln_g[0].reshape(1, CONF_W),
                  b_conf_ln_b[0].reshape(1, CONF_W), b_sc_dw[0], b_w_out[0].astype(BF16),
                  norm2_g[1].reshape(1, d), ffn_w1[1].astype(BF16), ffn_w3[1].astype(BF16),
                  ffn_w2[1].astype(BF16))
    return x
```

```python
import functools
import math

import numpy as np
import jax
import jax.numpy as jnp
from jax import lax
from jax.experimental import pallas as pl
from jax.experimental.pallas import tpu as pltpu

F32 = jnp.float32
BF16 = jnp.bfloat16

D_MODEL = 1024
CTX_LEN = 256
GRID_W = 64
EPS = 1e-6
MLA_HEADS = 8
QK_NOPE = 64
QK_ROPE = 32
QK_HEAD = QK_NOPE + QK_ROPE
V_HEAD = 64
Q_LORA = 384
KV_LORA = 256
ROPE_BASE = 10000.0
FNET_GROUPS = 4
FNET_GROUP_W = 128
FNET_W = FNET_GROUPS * FNET_GROUP_W
CONF_GROUPS = 4
CONF_W = 512
CONF_WIDTH = 31
SC_W = 512
SC_WIDTH = 3
D_FF = 2816
EVEN_KV_END = Q_LORA + KV_LORA + QK_ROPE

LANES = 128
HEAD_PAD = LANES
ROPE_HALF = QK_ROPE // 2
FFT_R = 64
HALO = 16
VMEM_LIMIT = 56 * 1024 * 1024

TM_EVEN_IN = 512
TQ_ATTN = 512
KC_ATTN = 512
TM_OUT = 512
TM_ODD = 512
FF_CHUNKS = tuple((c, min(512, D_FF - c)) for c in range(0, D_FF, 512))

Q_SCALE = QK_HEAD ** -0.5 * math.log2(math.e)


def _dot(a, b):
    return jnp.dot(a, b, preferred_element_type=F32)


def _dot_nt(a, b):
    return lax.dot_general(a, b, (((1,), (1,)), ((), ())), preferred_element_type=F32)


def _inv_rms(x, n):
    return lax.rsqrt(jnp.sum(x * x, axis=-1, keepdims=True) * (1.0 / n) + EPS)


def _norm_mod(x, g, shift, scale1p):
    return (x * _inv_rms(x, x.shape[-1]) * g) * scale1p + shift


def _silu(a):
    return a * jax.nn.sigmoid(a)


def _const_spec(shape):
    nd = len(shape)
    return pl.BlockSpec(shape, lambda *_: (0,) * nd, pipeline_mode=pl.Buffered(1))


def _params(sem, flags=None):
    return pltpu.CompilerParams(dimension_semantics=sem, vmem_limit_bytes=VMEM_LIMIT, flags=flags)


def _ada_kernel(c_ref, w_ref, b_ref, o_ref):
    s = _silu(c_ref[...]).astype(BF16)
    o_ref[0] = _dot(s, w_ref[0].astype(BF16)) + b_ref[0]


def _ada_call(cvec, ada_w, ada_b):
    depth, d, n6 = ada_w.shape
    rows = cvec.shape[0]
    tn = 1536
    return pl.pallas_call(
        _ada_kernel,
        grid=(depth, n6 // tn),
        in_specs=[
            pl.BlockSpec((rows, d), lambda l, j: (0, 0)),
            pl.BlockSpec((1, d, tn), lambda l, j: (l, 0, j)),
            pl.BlockSpec((1, 1, tn), lambda l, j: (l, 0, j)),
        ],
        out_specs=pl.BlockSpec((1, rows, tn), lambda l, j: (l, 0, j)),
        out_shape=jax.ShapeDtypeStruct((depth, rows, n6), F32),
        compiler_params=_params(("arbitrary", "arbitrary")),
        name="ada_mod",
    )(cvec, ada_w, ada_b.reshape(depth, 1, n6))


def _head_inv_rms(t, ones):
    ss = _dot((t * t).astype(BF16), ones)
    return lax.rsqrt(ss * (1.0 / QK_HEAD) + EPS)


def _keys_values(hb, krb, rope, wkv_ref, kvlg_ref, wuk_ref, wuv_ref, vone_ref, gk_ref, ones_ref, k_ref, v_ref):
    ckv = _dot(hb, wkv_ref[...])
    ckvn = (ckv * _inv_rms(ckv, KV_LORA) * kvlg_ref[...]).astype(BF16)
    kf = _dot(ckvn, wuk_ref[...])
    v_ref[0] = (_dot(ckvn, wuv_ref[...]) + vone_ref[...]).astype(BF16)
    gk = gk_ref[...]
    krg = krb * gk
    if rope is not None:
        rc, rs = rope
        lane = lax.broadcasted_iota(jnp.int32, krg.shape, 1)
        partner = jnp.where(lane < QK_NOPE + ROPE_HALF, pltpu.roll(krg, LANES - ROPE_HALF, 1),
                            pltpu.roll(krg, ROPE_HALF, 1))
        krg = krg * rc + partner * rs
    ones = ones_ref[...]
    for h in range(MLA_HEADS):
        kn = kf[:, h * HEAD_PAD:(h + 1) * HEAD_PAD]
        r = _head_inv_rms(kn + krb, ones)
        k_ref[0, :, h * HEAD_PAD:(h + 1) * HEAD_PAD] = ((kn * gk + krg) * r).astype(BF16)


def _even_in_kernel(x_ref, mod_ref, g_ref, wq_ref, wkv_ref, wf_ref, qlg_ref, kvlg_ref, wuq_ref, wuqs_ref, wuk_ref,
                    wuv_ref, vone_ref, gq_ref, gqs_ref, gk_ref, rc_ref, rs_ref, cs_ref, ones_ref,
                    q_ref, k_ref, v_ref, z_ref):
    hb = _norm_mod(x_ref[0], g_ref[...], mod_ref[0, 0:1, :], mod_ref[0, 1:2, :]).astype(BF16)
    rc, rs = rc_ref[...], rs_ref[...]
    uq = _dot(hb, wq_ref[...])
    _keys_values(hb, uq[:, Q_LORA:], (rc, rs), wkv_ref, kvlg_ref, wuk_ref, wuv_ref, vone_ref, gk_ref, ones_ref,
                 k_ref, v_ref)

    cq = uq[:, :Q_LORA]
    cqn = (cq * _inv_rms(cq, Q_LORA) * qlg_ref[...]).astype(BF16)
    qf = _dot(cqn, wuq_ref[...])
    qs = _dot(cqn, wuqs_ref[...])
    gq = gq_ref[...] * Q_SCALE
    gqs = gqs_ref[...] * Q_SCALE
    ones = ones_ref[...]
    for h in range(MLA_HEADS):
        sl = slice(h * HEAD_PAD, (h + 1) * HEAD_PAD)
        qh = qf[:, sl]
        r = _head_inv_rms(qh, ones)
        q_ref[0, :, sl] = ((qh * gq * rc + qs[:, sl] * gqs * rs) * r).astype(BF16)

    uf = _dot(hb, wf_ref[...]).astype(BF16)
    for g in range(FNET_GROUPS):
        zz = _dot(uf[:, g * LANES:(g + 1) * LANES], cs_ref[...])
        z_ref[0, :, g * LANES:(g + 1) * LANES] = zz[:, :LANES]
        z_ref[0, :, FNET_W + g * LANES:FNET_W + (g + 1) * LANES] = zz[:, LANES:]


def _ctx_in_kernel(x_ref, mod_ref, g_ref, wq_ref, wkv_ref, kvlg_ref, wuk_ref, wuv_ref, vone_ref, gk_ref, ones_ref,
                   k_ref, v_ref):
    hb = _norm_mod(x_ref[0], g_ref[...], mod_ref[0, 0:1, :], mod_ref[0, 1:2, :]).astype(BF16)
    krb = _dot(hb, wq_ref[:, Q_LORA:])
    _keys_values(hb, krb, None, wkv_ref, kvlg_ref, wuk_ref, wuv_ref, vone_ref, gk_ref, ones_ref, k_ref, v_ref)


def _even_in_call(x, mod, g, w, tables):
    bsz, n, d = x.shape
    tm = TM_EVEN_IN
    hw = MLA_HEADS * HEAD_PAD
    tok = lambda width: pl.BlockSpec((1, tm, width), lambda b, i: (b, i, 0))
    tab = pl.BlockSpec((tm, LANES), lambda b, i: (i, 0))
    return pl.pallas_call(
        _even_in_kernel,
        grid=(bsz, n // tm),
        in_specs=[
            tok(d),
            pl.BlockSpec((1, 8, d), lambda b, i: (b, 0, 0)),
            _const_spec((1, d)),
            _const_spec(w["wq"].shape), _const_spec(w["wkv"].shape), _const_spec(w["wf"].shape),
            _const_spec((1, Q_LORA)), _const_spec((1, KV_LORA)),
            _const_spec(w["wuq"].shape), _const_spec(w["wuqs"].shape), _const_spec(w["wuk"].shape),
            _const_spec(w["wuv"].shape), _const_spec((1, hw)),
            _const_spec((1, LANES)), _const_spec((1, LANES)), _const_spec((1, LANES)),
            tab, tab,
            _const_spec(tables["cs"].shape), _const_spec(tables["ones"].shape),
        ],
        out_specs=[tok(hw), tok(hw), tok(hw), tok(2 * FNET_W)],
        out_shape=[
            jax.ShapeDtypeStruct((bsz, n, hw), BF16),
            jax.ShapeDtypeStruct((bsz, n, hw), BF16),
            jax.ShapeDtypeStruct((bsz, n, hw), BF16),
            jax.ShapeDtypeStruct((bsz, n, 2 * FNET_W), F32),
        ],
        compiler_params=_params(("parallel", "parallel")),
        name="even_in",
    )(x, mod, g, w["wq"], w["wkv"], w["wf"], w["qlg"], w["kvlg"], w["wuq"], w["wuqs"], w["wuk"], w["wuv"],
      w["vone"], w["gq"], w["gqs"], w["gk"], tables["rc"], tables["rs"], tables["cs"], tables["ones"])


def _ctx_in_call(ctx, mod, g, w, tables):
    bsz, n, d = ctx.shape
    hw = MLA_HEADS * HEAD_PAD
    tok = lambda width: pl.BlockSpec((1, n, width), lambda b: (b, 0, 0))
    return pl.pallas_call(
        _ctx_in_kernel,
        grid=(bsz,),
        in_specs=[
            tok(d),
            pl.BlockSpec((1, 8, d), lambda b: (0, 0, 0)),
            _const_spec((1, d)),
            _const_spec(w["wq"].shape), _const_spec(w["wkv"].shape),
            _const_spec((1, KV_LORA)),
            _const_spec(w["wuk"].shape), _const_spec(w["wuv"].shape), _const_spec((1, hw)),
            _const_spec((1, LANES)), _const_spec(tables["ones"].shape),
        ],
        out_specs=[tok(hw), tok(hw)],
        out_shape=[
            jax.ShapeDtypeStruct((bsz, n, hw), BF16),
            jax.ShapeDtypeStruct((bsz, n, hw), BF16),
        ],
        compiler_params=_params(("parallel",)),
        name="ctx_in",
    )(ctx, mod, g, w["wq"], w["wkv"], w["kvlg"], w["wuk"], w["wuv"], w["vone"], w["gk"], tables["ones"])


DFT_NB = 8


def _dft1_kernel(z_ref, are_ref, aim_ref, t_ref):
    rows = z_ref.shape[1] * DFT_NB
    z = z_ref[0].reshape(rows, 2 * FNET_W).astype(BF16)
    t = _dot(are_ref[...], z[:, :FNET_W]) + _dot(aim_ref[...], z[:, FNET_W:])
    t_ref[0] = t.reshape(t_ref.shape[1:])


def _dft2_kernel(tr_ref, ti_ref, bre_ref, bim_ref, y_ref):
    rows = DFT_NB * tr_ref.shape[2]
    xr = tr_ref[0].reshape(rows, FNET_W).astype(BF16)
    xi = ti_ref[0].reshape(rows, FNET_W).astype(BF16)
    y = _dot(bre_ref[0], xr) + _dot(bim_ref[0], xi)
    y_ref[0] = y.reshape(y_ref.shape[1:])


def _seq_dft_call(z, tables):
    bsz, n, w2 = z.shape
    r = FFT_R
    nb = DFT_NB
    t = pl.pallas_call(
        _dft1_kernel,
        grid=(bsz, r // nb),
        in_specs=[
            pl.BlockSpec((1, r, nb, w2), lambda b, j: (b, 0, j, 0)),
            _const_spec(tables["a_re"].shape), _const_spec(tables["a_im"].shape),
        ],
        out_specs=pl.BlockSpec((1, 2 * r, nb, FNET_W), lambda b, j: (b, 0, j, 0)),
        out_shape=jax.ShapeDtypeStruct((bsz, 2 * r, r, FNET_W), F32),
        compiler_params=_params(("parallel", "parallel")),
        name="seq_dft1",
    )(z.reshape(bsz, r, r, w2), tables["a_re"], tables["a_im"])
    rows = nb * r
    y = pl.pallas_call(
        _dft2_kernel,
        grid=(r // nb, bsz),
        in_specs=[
            pl.BlockSpec((1, nb, r, FNET_W), lambda j, b: (b, j, 0, 0)),
            pl.BlockSpec((1, nb, r, FNET_W), lambda j, b: (b, r // nb + j, 0, 0)),
            pl.BlockSpec((1, rows, rows), lambda j, b: (j, 0, 0)),
            pl.BlockSpec((1, rows, rows), lambda j, b: (j, 0, 0)),
        ],
        out_specs=pl.BlockSpec((1, r, nb, FNET_W), lambda j, b: (b, 0, j, 0)),
        out_shape=jax.ShapeDtypeStruct((bsz, r, r, FNET_W), F32),
        compiler_params=_params(("parallel", "parallel")),
        name="seq_dft2",
    )(t, t, tables["b_re"], tables["b_im"])
    return y.reshape(bsz, n, FNET_W)


def _attn_kernel(q_ref, kc_ref, k_ref, vc_ref, v_ref, o_ref, s0_ref, s1_ref, m0_ref, m1_ref):
    n_ctx = kc_ref.shape[1]
    n_chunks = k_ref.shape[1] // KC_ATTN
    tq = TQ_ATTN
    n_tiles = q_ref.shape[1] // tq
    s_refs = (s0_ref, s1_ref)
    m_refs = (m0_ref, m1_ref)
    chunks = [(kc_ref, vc_ref, 0, n_ctx, 0)] + [
        (k_ref, v_ref, j * KC_ATTN, KC_ATTN, n_ctx + j * KC_ATTN) for j in range(n_chunks)]

    def lane_groups(x):
        return [x[:, t * LANES:(t + 1) * LANES] for t in range(x.shape[1] // LANES)]

    def segment(i, h_scores, h_values):
        lo = h_scores * HEAD_PAD
        q = q_ref[0, pl.ds(pl.multiple_of(i * tq, tq), tq), lo:lo + HEAD_PAD]
        s_out = s_refs[h_scores]
        m_run = acc = None
        if h_values is not None:
            s_in = s_refs[h_values]
            m_in = m_refs[h_values][...]
            lv = h_values * HEAD_PAD
        for kr, vr, r0, width, c0 in chunks:
            if h_values is not None:
                p = [jnp.exp2(g - m_in) for g in lane_groups(s_in[:, c0:c0 + width])]
                pv = _dot(jnp.concatenate(p, axis=1).astype(BF16), vr[0, r0:r0 + width, lv:lv + HEAD_PAD])
                acc = pv if acc is None else acc + pv
            s = _dot_nt(q, kr[0, r0:r0 + width, lo:lo + HEAD_PAD])
            s_out[:, c0:c0 + width] = s
            m_run = functools.reduce(jnp.maximum, lane_groups(s) if m_run is None else [m_run] + lane_groups(s))
        m_refs[h_scores][...] = jnp.broadcast_to(jnp.max(m_run, axis=-1, keepdims=True), (tq, LANES))
        return acc

    segment(0, 0, None)

    def body(i, carry):
        acc0 = segment(i, 1, 0)
        acc1 = segment(jnp.minimum(i + 1, n_tiles - 1), 0, 1)
        out0 = acc0 / acc0[:, V_HEAD:V_HEAD + 1]
        out1 = acc1 / acc1[:, 0:1]
        lane = lax.broadcasted_iota(jnp.int32, out0.shape, 1)
        o_ref[0, pl.ds(pl.multiple_of(i * tq, tq), tq), :] = jnp.where(lane < V_HEAD, out0, out1).astype(BF16)
        return carry

    lax.fori_loop(0, n_tiles, body, 0)


def _attn_call(q, k_ctx, k, v_ctx, v):
    bsz, n, hw = q.shape
    n_ctx = k_ctx.shape[1]
    tq = TQ_ATTN
    pair = 2 * HEAD_PAD
    return pl.pallas_call(
        _attn_kernel,
        grid=(bsz, MLA_HEADS // 2),
        in_specs=[
            pl.BlockSpec((1, n, pair), lambda b, h: (b, 0, h)),
            pl.BlockSpec((1, n_ctx, pair), lambda b, h: (b, 0, h)),
            pl.BlockSpec((1, n, pair), lambda b, h: (b, 0, h)),
            pl.BlockSpec((1, n_ctx, pair), lambda b, h: (b, 0, h)),
            pl.BlockSpec((1, n, pair), lambda b, h: (b, 0, h)),
        ],
        out_specs=pl.BlockSpec((1, n, 2 * V_HEAD), lambda b, h: (b, 0, h)),
        out_shape=jax.ShapeDtypeStruct((bsz, n, MLA_HEADS * V_HEAD), BF16),
        scratch_shapes=[pltpu.VMEM((tq, n_ctx + n), F32), pltpu.VMEM((tq, n_ctx + n), F32),
                        pltpu.VMEM((tq, LANES), F32), pltpu.VMEM((tq, LANES), F32)],
        compiler_params=_params(("parallel", "parallel")),
        name="attention",
    )(q, k_ctx, k, v_ctx, v)


def _ffn_tail(x1, mod_ref, g2n_ref, w1_ref, w3_ref, w2_ref, out_ref):
    h2 = _norm_mod(x1, g2n_ref[...], mod_ref[0, 3:4, :], mod_ref[0, 4:5, :]).astype(BF16)
    acc = None
    for c0, cw in FF_CHUNKS:
        a = _dot(h2, w1_ref[:, c0:c0 + cw])
        b = _dot(h2, w3_ref[:, c0:c0 + cw])
        part = _dot((_silu(a) * b).astype(BF16), w2_ref[c0:c0 + cw, :])
        acc = part if acc is None else acc + part
    out_ref[0] = x1 + mod_ref[0, 5:6, :] * acc


def _even_out_kernel(x_ref, o_ref, y_ref, mod_ref, g2n_ref, wo_ref, w1_ref, w3_ref, w2_ref, out_ref):
    hv = MLA_HEADS * V_HEAD
    mix = _dot(o_ref[0], wo_ref[0:hv, :]) + _dot(y_ref[0].astype(BF16), wo_ref[hv:, :])
    x1 = x_ref[0] + mod_ref[0, 2:3, :] * mix
    _ffn_tail(x1, mod_ref, g2n_ref, w1_ref, w3_ref, w2_ref, out_ref)


def _even_out_call(x, o, y, mod, g2n, wo, w1, w3, w2):
    bsz, n, d = x.shape
    tm = TM_OUT
    tok = lambda width: pl.BlockSpec((1, tm, width), lambda b, i: (b, i, 0))
    return pl.pallas_call(
        _even_out_kernel,
        grid=(bsz, n // tm),
        in_specs=[
            tok(d), tok(o.shape[-1]), tok(y.shape[-1]),
            pl.BlockSpec((1, 8, d), lambda b, i: (b, 0, 0)),
            _const_spec((1, d)),
            _const_spec(wo.shape), _const_spec(w1.shape), _const_spec(w3.shape), _const_spec(w2.shape),
        ],
        out_specs=tok(d),
        out_shape=jax.ShapeDtypeStruct((bsz, n, d), F32),
        compiler_params=_params(("parallel", "parallel")),
        name="even_out_ffn",
    )(x, o, y, mod, g2n, wo, w1, w3, w2)


CONV_RB = 64


SUBLANES = 8


def _conf_conv_block(cbuf, dw_ref, bias, r0, ls):
    rows = CONV_RB + SUBLANES
    acc = jnp.broadcast_to(bias, (CONV_RB, LANES))
    for r in range(SUBLANES):
        part = None
        for o in range(r, CONF_WIDTH + 1, SUBLANES):
            if o == 0:
                continue
            term = dw_ref[o - 1:o, ls] * cbuf[r0 + o - r:r0 + o - r + rows, ls]
            part = term if part is None else part + term
        acc = acc + part[r:r + CONV_RB]
    return acc


def _odd_kernel(xp_ref, x_ref, xn_ref, mod_ref, g1n_ref, win_ref, dw_ref, dwb_ref, lng_ref, lnb_ref, scw_ref,
                wout_ref, g2n_ref, w1_ref, w3_ref, w2_ref, out_ref, hb_ref, cbuf, sbuf, mbuf):
    tm = x_ref.shape[1]
    i = pl.program_id(1)
    last = pl.num_programs(1) - 1
    g = g1n_ref[...]
    sh = mod_ref[0, 0:1, :]
    sc = mod_ref[0, 1:2, :]
    x = x_ref[0]
    hb_ref[0:HALO, :] = _norm_mod(xp_ref[0], g, sh, sc).astype(BF16)
    hb_ref[HALO:HALO + tm, :] = _norm_mod(x, g, sh, sc).astype(BF16)
    hb_ref[HALO + tm:, :] = _norm_mod(xn_ref[0], g, sh, sc).astype(BF16)
    u = _dot(hb_ref[...], win_ref[...])

    conf = u[:, 0:CONF_W] * jax.nn.sigmoid(u[:, CONF_W:2 * CONF_W])
    cx = u[:, 2 * CONF_W + SC_W:2 * CONF_W + 2 * SC_W] * u[:, 2 * CONF_W + 2 * SC_W:]
    keep_lo = i > 0
    keep_hi = i < last
    cbuf[0:HALO, :] = jnp.where(keep_lo, conf[0:HALO], 0.0)
    cbuf[HALO:HALO + tm, :] = conf[HALO:HALO + tm]
    cbuf[HALO + tm:, :] = jnp.where(keep_hi, conf[HALO + tm:], 0.0)
    sbuf[0:HALO, :] = jnp.where(keep_lo, cx[0:HALO], 0.0)
    sbuf[HALO:HALO + tm, :] = cx[HALO:HALO + tm]
    sbuf[HALO + tm:, :] = jnp.where(keep_hi, cx[HALO + tm:], 0.0)
    sb = u[HALO:HALO + tm, 2 * CONF_W:2 * CONF_W + SC_W]

    assert HALO == (CONF_WIDTH - 1) // 2 + 1
    pad_s = (SC_WIDTH - 1) // 2
    for rb in range(tm // CONV_RB):
        r0 = rb * CONV_RB
        for lg in range(CONF_W // LANES):
            ls = slice(lg * LANES, (lg + 1) * LANES)
            acc = _conf_conv_block(cbuf, dw_ref, dwb_ref[:, ls], r0, ls)
            mu = jnp.mean(acc, axis=-1, keepdims=True)
            cen = acc - mu
            var = jnp.mean(cen * cen, axis=-1, keepdims=True)
            yn = cen * lax.rsqrt(var + EPS) * lng_ref[:, ls] + lnb_ref[:, ls]
            mbuf[r0:r0 + CONV_RB, ls] = _silu(yn).astype(BF16)

            acc = None
            for j in range(SC_WIDTH):
                o = HALO + r0 + j - pad_s
                term = scw_ref[j:j + 1, ls] * sbuf[o:o + CONV_RB, ls]
                acc = term if acc is None else acc + term
            mbuf[r0:r0 + CONV_RB, CONF_W + lg * LANES:CONF_W + (lg + 1) * LANES] = (
                sb[r0:r0 + CONV_RB, ls] * acc).astype(BF16)

    mix = _dot(mbuf[...], wout_ref[...])
    x1 = x + mod_ref[0, 2:3, :] * mix
    _ffn_tail(x1, mod_ref, g2n_ref, w1_ref, w3_ref, w2_ref, out_ref)


def _odd_call(x, mod, g1n, win, dw, dwb, lng, lnb, scw, wout, g2n, w1, w3, w2):
    bsz, n, d = x.shape
    tm = TM_ODD
    hb = tm // HALO
    nh = n // HALO
    return pl.pallas_call(
        _odd_kernel,
        grid=(bsz, n // tm),
        in_specs=[
            pl.BlockSpec((1, HALO, d), lambda b, i: (b, jnp.maximum(i * hb - 1, 0), 0)),
            pl.BlockSpec((1, tm, d), lambda b, i: (b, i, 0)),
            pl.BlockSpec((1, HALO, d), lambda b, i: (b, jnp.minimum((i + 1) * hb, nh - 1), 0)),
            pl.BlockSpec((1, 8, d), lambda b, i: (b, 0, 0)),
            _const_spec((1, d)),
            _const_spec(win.shape),
            _const_spec(dw.shape), _const_spec(dwb.shape), _const_spec(lng.shape), _const_spec(lnb.shape),
            _const_spec(scw.shape),
            _const_spec(wout.shape),
            _const_spec((1, d)),
            _const_spec(w1.shape), _const_spec(w3.shape), _const_spec(w2.shape),
        ],
        out_specs=pl.BlockSpec((1, tm, d), lambda b, i: (b, i, 0)),
        out_shape=jax.ShapeDtypeStruct((bsz, n, d), F32),
        scratch_shapes=[
            pltpu.VMEM((tm + 2 * HALO, d), BF16),
            pltpu.VMEM((tm + 2 * HALO, CONF_W), F32),
            pltpu.VMEM((tm + 2 * HALO, SC_W), F32),
            pltpu.VMEM((tm, CONF_W + SC_W), BF16),
        ],
        compiler_params=_params(("parallel", "arbitrary")),
        name="odd_layer",
    )(x, x, x, mod, g1n, win, dw, dwb, lng, lnb, scw, wout, g2n, w1, w3, w2)


def _dft_tables():
    r = FFT_R
    gw = FNET_GROUP_W
    idx = np.arange(gw)
    ang = 2.0 * np.pi * ((idx[:, None] * idx[None, :]) % gw) / gw
    cs = np.concatenate([np.cos(ang), -np.sin(ang)], axis=1) / np.sqrt(gw)
    a = np.arange(r)
    nb = DFT_NB
    eye = np.eye(nb)
    ang1 = 2.0 * np.pi * ((a[:, None] * a[None, :]) % r) / r
    fr, fi = np.cos(ang1) / 8.0, -np.sin(ang1) / 8.0
    a_re = np.concatenate([np.kron(fr, eye), np.kron(fi, eye)], axis=0)
    a_im = np.concatenate([np.kron(-fi, eye), np.kron(fr, eye)], axis=0)
    n = r * r
    c = a[:, None, None]
    dd = a[None, :, None]
    bb = a[None, None, :]
    ang2 = 2.0 * np.pi * ((bb * (c + r * dd)) % n) / n
    gr, gi_neg = np.cos(ang2) / 8.0, np.sin(ang2) / 8.0
    b_re = np.zeros((r // nb, r, nb, nb, r))
    b_im = np.zeros((r // nb, r, nb, nb, r))
    for ci in range(nb):
        b_re[:, :, ci, ci, :] = gr[ci::nb]
        b_im[:, :, ci, ci, :] = gi_neg[ci::nb]
    b_re = b_re.reshape(r // nb, r * nb, nb * r)
    b_im = b_im.reshape(r // nb, r * nb, nb * r)
    named = (("cs", cs), ("a_re", a_re), ("a_im", a_im), ("b_re", b_re), ("b_im", b_im))
    return {name: jnp.asarray(t, F32).astype(BF16) for name, t in named}


def _rope_tables(n):
    rows = n // GRID_W
    row = jnp.broadcast_to(jnp.arange(rows, dtype=F32)[:, None], (rows, GRID_W)).reshape(n)
    col = jnp.broadcast_to(jnp.arange(GRID_W, dtype=F32)[None, :], (rows, GRID_W)).reshape(n)
    per_axis = QK_ROPE // 4
    inv_freq = ROPE_BASE ** (-jnp.arange(per_axis, dtype=F32) / per_axis)
    ang = jnp.concatenate([row[:, None] * inv_freq, col[:, None] * inv_freq], axis=-1)
    cos, sin = jnp.cos(ang), jnp.sin(ang)
    tail = LANES - QK_HEAD
    rc = jnp.concatenate([jnp.ones((n, QK_NOPE), F32), cos, cos, jnp.ones((n, tail), F32)], axis=-1)
    rs = jnp.concatenate([jnp.zeros((n, QK_NOPE), F32), -sin, sin, jnp.zeros((n, tail), F32)], axis=-1)
    return rc, rs


def _swap_rope_halves(t):
    r1 = t[..., QK_NOPE:QK_NOPE + ROPE_HALF]
    r2 = t[..., QK_NOPE + ROPE_HALF:]
    return jnp.concatenate([jnp.zeros_like(t[..., :QK_NOPE]), r2, r1], axis=-1)


def _pad_heads(w, width):
    r = w.shape[0]
    return jnp.pad(w, ((0, 0), (0, 0), (0, HEAD_PAD - width))).reshape(r, MLA_HEADS * HEAD_PAD)


def _pad_gain(g):
    return jnp.pad(g, (0, HEAD_PAD - QK_HEAD)).reshape(1, HEAD_PAD)


def _pad_values(w):
    r = w.shape[0]
    pairs = w.reshape(r, MLA_HEADS // 2, 2, V_HEAD)
    zero = jnp.zeros_like(pairs[:, :, 0])
    even = jnp.concatenate([pairs[:, :, 0], zero], axis=-1)
    odd = jnp.concatenate([zero, pairs[:, :, 1]], axis=-1)
    return jnp.stack([even, odd], axis=2).reshape(r, MLA_HEADS * HEAD_PAD)


def _value_ones_row():
    row = np.zeros((MLA_HEADS, HEAD_PAD), np.float32)
    row[0::2, V_HEAD] = 1.0
    row[1::2, 0] = 1.0
    return jnp.asarray(row.reshape(1, MLA_HEADS * HEAD_PAD))


def kernel(x, c, ctx, c_ctx, ada_w, ada_b, norm1_g, norm2_g, ffn_w1, ffn_w3, ffn_w2, a_w_in, a_q_ln_g, a_kv_ln_g, a_w_uq, a_w_uk, a_w_uv, a_q_norm_g, a_k_norm_g, a_w_out, b_w_in, b_conf_dw, b_conf_dw_b, b_conf_ln_g, b_conf_ln_b, b_sc_dw, b_w_out):
    bsz, n, d = x.shape
    depth = ada_w.shape[0]
    assert depth == 2 and d == D_MODEL and n == FFT_R * FFT_R and ctx.shape[1] == CTX_LEN

    rows = 16
    cvec = jnp.concatenate([c, c_ctx[None, :], jnp.zeros((rows - bsz - 1, d), F32)], axis=0)
    ada = _ada_call(cvec, ada_w, ada_b).reshape(depth, rows, 6, d)
    unit = jnp.array([0.0, 1.0, 0.0, 0.0, 1.0, 0.0], F32)[None, None, :, None]
    mods = jnp.pad(ada + unit, ((0, 0), (0, 0), (0, 2), (0, 0)))

    tables = _dft_tables()
    tables["rc"], tables["rs"] = _rope_tables(n)
    tables["ones"] = jnp.ones((LANES, LANES), BF16)

    w_in = a_w_in[0]
    kr_cols = jnp.pad(w_in[:, Q_LORA + KV_LORA:EVEN_KV_END], ((0, 0), (QK_NOPE, LANES - QK_HEAD)))
    w = {
        "wq": jnp.concatenate([w_in[:, :Q_LORA], kr_cols], axis=1).astype(BF16),
        "wkv": w_in[:, Q_LORA:Q_LORA + KV_LORA].astype(BF16),
        "wf": w_in[:, EVEN_KV_END:].astype(BF16),
        "qlg": a_q_ln_g[0].reshape(1, Q_LORA),
        "kvlg": a_kv_ln_g[0].reshape(1, KV_LORA),
        "wuq": _pad_heads(a_w_uq[0], QK_HEAD).astype(BF16),
        "wuqs": _pad_heads(_swap_rope_halves(a_w_uq[0]), QK_HEAD).astype(BF16),
        "wuk": _pad_heads(a_w_uk[0], QK_NOPE).astype(BF16),
        "wuv": _pad_values(a_w_uv[0]).astype(BF16),
        "vone": _value_ones_row(),
        "gq": _pad_gain(a_q_norm_g[0]),
        "gqs": _pad_gain(_swap_rope_halves(a_q_norm_g[0])),
        "gk": _pad_gain(a_k_norm_g[0]),
    }
    g1 = norm1_g[0].reshape(1, d)
    q, k, v, z = _even_in_call(x, mods[0, :bsz], g1, w, tables)
    k_ctx, v_ctx = _ctx_in_call(ctx, mods[0, bsz:bsz + 1], g1, w, tables)
    y = _seq_dft_call(z, tables)
    o = _attn_call(q, k_ctx, k, v_ctx, v)
    x = _even_out_call(x, o, y, mods[0, :bsz], norm2_g[0].reshape(1, d), a_w_out[0].astype(BF16),
                       ffn_w1[0].astype(BF16), ffn_w3[0].astype(BF16), ffn_w2[0].astype(BF16))

    x = _odd_call(x, mods[1, :bsz], norm1_g[1].reshape(1, d), b_w_in[0].astype(BF16),
                  b_conf_dw[0], b_conf_dw_b[0].reshape(1, CONF_W), b_conf_ln_g[0].reshape(1, CONF_W),
                  b_conf_ln_b[0].reshape(1, CONF_W), b_sc_dw[0], b_w_out[0].astype(BF16),
                  norm2_g[1].reshape(1, d), ffn_w1[1].astype(BF16), ffn_w3[1].astype(BF16),
                  ffn_w2[1].astype(BF16))
    return x
```

```python
import functools
import math

import numpy as np
import jax
import jax.numpy as jnp
from jax import lax
from jax.experimental import pallas as pl
from jax.experimental.pallas import tpu as pltpu

F32 = jnp.float32
BF16 = jnp.bfloat16

D_MODEL = 1024
CTX_LEN = 256
GRID_W = 64
EPS = 1e-6
MLA_HEADS = 8
QK_NOPE = 64
QK_ROPE = 32
QK_HEAD = QK_NOPE + QK_ROPE
V_HEAD = 64
Q_LORA = 384
KV_LORA = 256
ROPE_BASE = 10000.0
FNET_GROUPS = 4
FNET_GROUP_W = 128
FNET_W = FNET_GROUPS * FNET_GROUP_W
CONF_GROUPS = 4
CONF_W = 512
CONF_WIDTH = 31
SC_W = 512
SC_WIDTH = 3
D_FF = 2816
EVEN_KV_END = Q_LORA + KV_LORA + QK_ROPE

LANES = 128
HEAD_PAD = LANES
ROPE_HALF = QK_ROPE // 2
FFT_R = 64
HALO = 16
VMEM_LIMIT = 56 * 1024 * 1024

TM_EVEN_IN = 512
TQ_ATTN = 512
KC_ATTN = 512
TM_OUT = 512
TM_ODD = 512
FF_CHUNKS = tuple((c, min(512, D_FF - c)) for c in range(0, D_FF, 512))

Q_SCALE = QK_HEAD ** -0.5 * math.log2(math.e)


def _dot(a, b):
    return jnp.dot(a, b, preferred_element_type=F32)


def _dot_nt(a, b):
    return lax.dot_general(a, b, (((1,), (1,)), ((), ())), preferred_element_type=F32)


def _inv_rms(x, n):
    return lax.rsqrt(jnp.sum(x * x, axis=-1, keepdims=True) * (1.0 / n) + EPS)


def _norm_mod(x, g, shift, scale1p):
    return (x * _inv_rms(x, x.shape[-1]) * g) * scale1p + shift


def _silu(a):
    return a * jax.nn.sigmoid(a)


def _const_spec(shape):
    nd = len(shape)
    return pl.BlockSpec(shape, lambda *_: (0,) * nd, pipeline_mode=pl.Buffered(1))


def _params(sem, flags=None):
    return pltpu.CompilerParams(dimension_semantics=sem, vmem_limit_bytes=VMEM_LIMIT, flags=flags)


def _ada_kernel(c_ref, w_ref, b_ref, o_ref):
    s = _silu(c_ref[...]).astype(BF16)
    o_ref[0] = _dot(s, w_ref[0].astype(BF16)) + b_ref[0]


def _ada_call(cvec, ada_w, ada_b):
    depth, d, n6 = ada_w.shape
    rows = cvec.shape[0]
    tn = 1536
    return pl.pallas_call(
        _ada_kernel,
        grid=(depth, n6 // tn),
        in_specs=[
            pl.BlockSpec((rows, d), lambda l, j: (0, 0)),
            pl.BlockSpec((1, d, tn), lambda l, j: (l, 0, j)),
            pl.BlockSpec((1, 1, tn), lambda l, j: (l, 0, j)),
        ],
        out_specs=pl.BlockSpec((1, rows, tn), lambda l, j: (l, 0, j)),
        out_shape=jax.ShapeDtypeStruct((depth, rows, n6), F32),
        compiler_params=_params(("arbitrary", "arbitrary")),
        name="ada_mod",
    )(cvec, ada_w, ada_b.reshape(depth, 1, n6))


def _head_inv_rms(t, ones):
    ss = _dot((t * t).astype(BF16), ones)
    return lax.rsqrt(ss * (1.0 / QK_HEAD) + EPS)


def _keys_values(hb, krb, rope, wkv_ref, kvlg_ref, wuk_ref, wuv_ref, vone_ref, gk_ref, ones_ref, k_ref, v_ref):
    ckv = _dot(hb, wkv_ref[...])
    ckvn = (ckv * _inv_rms(ckv, KV_LORA) * kvlg_ref[...]).astype(BF16)
    kf = _dot(ckvn, wuk_ref[...])
    v_ref[0] = (_dot(ckvn, wuv_ref[...]) + vone_ref[...]).astype(BF16)
    gk = gk_ref[...]
    krg = krb * gk
    if rope is not None:
        rc, rs = rope
        lane = lax.broadcasted_iota(jnp.int32, krg.shape, 1)
        partner = jnp.where(lane < QK_NOPE + ROPE_HALF, pltpu.roll(krg, LANES - ROPE_HALF, 1),
                            pltpu.roll(krg, ROPE_HALF, 1))
        krg = krg * rc + partner * rs
    ones = ones_ref[...]
    for h in range(MLA_HEADS):
        kn = kf[:, h * HEAD_PAD:(h + 1) * HEAD_PAD]
        r = _head_inv_rms(kn + krb, ones)
        k_ref[0, :, h * HEAD_PAD:(h + 1) * HEAD_PAD] = ((kn * gk + krg) * r).astype(BF16)


def _even_in_kernel(x_ref, mod_ref, g_ref, wq_ref, wkv_ref, wf_ref, qlg_ref, kvlg_ref, wuq_ref, wuqs_ref, wuk_ref,
                    wuv_ref, vone_ref, gq_ref, gqs_ref, gk_ref, rc_ref, rs_ref, cs_ref, ones_ref,
                    q_ref, k_ref, v_ref, z_ref):
    hb = _norm_mod(x_ref[0], g_ref[...], mod_ref[0, 0:1, :], mod_ref[0, 1:2, :]).astype(BF16)
    rc, rs = rc_ref[...], rs_ref[...]
    uq = _dot(hb, wq_ref[...])
    _keys_values(hb, uq[:, Q_LORA:], (rc, rs), wkv_ref, kvlg_ref, wuk_ref, wuv_ref, vone_ref, gk_ref, ones_ref,
                 k_ref, v_ref)

    cq = uq[:, :Q_LORA]
    cqn = (cq * _inv_rms(cq, Q_LORA) * qlg_ref[...]).astype(BF16)
    qf = _dot(cqn, wuq_ref[...])
    qs = _dot(cqn, wuqs_ref[...])
    gq = gq_ref[...] * Q_SCALE
    gqs = gqs_ref[...] * Q_SCALE
    ones = ones_ref[...]
    for h in range(MLA_HEADS):
        sl = slice(h * HEAD_PAD, (h + 1) * HEAD_PAD)
        qh = qf[:, sl]
        r = _head_inv_rms(qh, ones)
        q_ref[0, :, sl] = ((qh * gq * rc + qs[:, sl] * gqs * rs) * r).astype(BF16)

    uf = _dot(hb, wf_ref[...]).astype(BF16)
    for g in range(FNET_GROUPS):
        zz = _dot(uf[:, g * LANES:(g + 1) * LANES], cs_ref[...])
        z_ref[0, :, g * LANES:(g + 1) * LANES] = zz[:, :LANES]
        z_ref[0, :, FNET_W + g * LANES:FNET_W + (g + 1) * LANES] = zz[:, LANES:]


def _ctx_in_kernel(x_ref, mod_ref, g_ref, wq_ref, wkv_ref, kvlg_ref, wuk_ref, wuv_ref, vone_ref, gk_ref, ones_ref,
                   k_ref, v_ref):
    hb = _norm_mod(x_ref[0], g_ref[...], mod_ref[0, 0:1, :], mod_ref[0, 1:2, :]).astype(BF16)
    krb = _dot(hb, wq_ref[:, Q_LORA:])
    _keys_values(hb, krb, None, wkv_ref, kvlg_ref, wuk_ref, wuv_ref, vone_ref, gk_ref, ones_ref, k_ref, v_ref)


def _even_in_call(x, mod, g, w, tables):
    bsz, n, d = x.shape
    tm = TM_EVEN_IN
    hw = MLA_HEADS * HEAD_PAD
    tok = lambda width: pl.BlockSpec((1, tm, width), lambda b, i: (b, i, 0))
    tab = pl.BlockSpec((tm, LANES), lambda b, i: (i, 0))
    return pl.pallas_call(
        _even_in_kernel,
        grid=(bsz, n // tm),
        in_specs=[
            tok(d),
            pl.BlockSpec((1, 8, d), lambda b, i: (b, 0, 0)),
            _const_spec((1, d)),
            _const_spec(w["wq"].shape), _const_spec(w["wkv"].shape), _const_spec(w["wf"].shape),
            _const_spec((1, Q_LORA)), _const_spec((1, KV_LORA)),
            _const_spec(w["wuq"].shape), _const_spec(w["wuqs"].shape), _const_spec(w["wuk"].shape),
            _const_spec(w["wuv"].shape), _const_spec((1, hw)),
            _const_spec((1, LANES)), _const_spec((1, LANES)), _const_spec((1, LANES)),
            tab, tab,
            _const_spec(tables["cs"].shape), _const_spec(tables["ones"].shape),
        ],
        out_specs=[tok(hw), tok(hw), tok(hw), tok(2 * FNET_W)],
        out_shape=[
            jax.ShapeDtypeStruct((bsz, n, hw), BF16),
            jax.ShapeDtypeStruct((bsz, n, hw), BF16),
            jax.ShapeDtypeStruct((bsz, n, hw), BF16),
            jax.ShapeDtypeStruct((bsz, n, 2 * FNET_W), F32),
        ],
        compiler_params=_params(("parallel", "parallel")),
        name="even_in",
    )(x, mod, g, w["wq"], w["wkv"], w["wf"], w["qlg"], w["kvlg"], w["wuq"], w["wuqs"], w["wuk"], w["wuv"],
      w["vone"], w["gq"], w["gqs"], w["gk"], tables["rc"], tables["rs"], tables["cs"], tables["ones"])


def _ctx_in_call(ctx, mod, g, w, tables):
    bsz, n, d = ctx.shape
    hw = MLA_HEADS * HEAD_PAD
    tok = lambda width: pl.BlockSpec((1, n, width), lambda b: (b, 0, 0))
    return pl.pallas_call(
        _ctx_in_kernel,
        grid=(bsz,),
        in_specs=[
            tok(d),
            pl.BlockSpec((1, 8, d), lambda b: (0, 0, 0)),
            _const_spec((1, d)),
            _const_spec(w["wq"].shape), _const_spec(w["wkv"].shape),
            _const_spec((1, KV_LORA)),
            _const_spec(w["wuk"].shape), _const_spec(w["wuv"].shape), _const_spec((1, hw)),
            _const_spec((1, LANES)), _const_spec(tables["ones"].shape),
        ],
        out_specs=[tok(hw), tok(hw)],
        out_shape=[
            jax.ShapeDtypeStruct((bsz, n, hw), BF16),
            jax.ShapeDtypeStruct((bsz, n, hw), BF16),
        ],
        compiler_params=_params(("parallel",)),
        name="ctx_in",
    )(ctx, mod, g, w["wq"], w["wkv"], w["kvlg"], w["wuk"], w["wuv"], w["vone"], w["gk"], tables["ones"])


DFT_NB = 8


def _seq_dft_kernel(z_ref, are_ref, aim_ref, bre_ref, bim_ref, y_ref, t_scr):
    r = FFT_R
    half = r // DFT_NB
    s = pl.program_id(1)

    @pl.when(s < half)
    def _():
        z = z_ref[0].reshape(r * DFT_NB, 2 * FNET_W).astype(BF16)
        t = _dot(are_ref[...], z[:, :FNET_W]) + _dot(aim_ref[...], z[:, FNET_W:])
        t_scr[:, pl.ds(pl.multiple_of(s * DFT_NB, DFT_NB), DFT_NB), :] = t.reshape(2 * r, DFT_NB, FNET_W)

    @pl.when(s >= half)
    def _():
        j = s - half
        c0 = pl.multiple_of(j * DFT_NB, DFT_NB)
        xr = t_scr[pl.ds(c0, DFT_NB)].reshape(DFT_NB * r, FNET_W).astype(BF16)
        xi = t_scr[pl.ds(r + c0, DFT_NB)].reshape(DFT_NB * r, FNET_W).astype(BF16)
        y = _dot(bre_ref[j], xr) + _dot(bim_ref[j], xi)
        y_ref[0] = y.reshape(y_ref.shape[1:])


def _seq_dft_call(z, tables):
    bsz, n, w2 = z.shape
    r = FFT_R
    nb = DFT_NB
    half = r // nb
    y = pl.pallas_call(
        _seq_dft_kernel,
        grid=(bsz, 2 * half),
        in_specs=[
            pl.BlockSpec((1, r, nb, w2), lambda b, s: (b, 0, jnp.minimum(s, half - 1), 0)),
            _const_spec(tables["a_re"].shape), _const_spec(tables["a_im"].shape),
            _const_spec(tables["b_re"].shape), _const_spec(tables["b_im"].shape),
        ],
        out_specs=pl.BlockSpec((1, r, nb, FNET_W), lambda b, s: (b, 0, jnp.maximum(s - half, 0), 0)),
        out_shape=jax.ShapeDtypeStruct((bsz, r, r, FNET_W), F32),
        scratch_shapes=[pltpu.VMEM((2 * r, r, FNET_W), F32)],
        compiler_params=_params(("parallel", "arbitrary")),
        name="seq_dft",
    )(z.reshape(bsz, r, r, w2), tables["a_re"], tables["a_im"], tables["b_re"], tables["b_im"])
    return y.reshape(bsz, n, FNET_W)


def _attn_kernel(q_ref, kc_ref, k_ref, vc_ref, v_ref, o_ref, s0_ref, s1_ref, m0_ref, m1_ref):
    n_ctx = kc_ref.shape[1]
    n_chunks = k_ref.shape[1] // KC_ATTN
    tq = TQ_ATTN
    n_tiles = q_ref.shape[1] // tq
    s_refs = (s0_ref, s1_ref)
    m_refs = (m0_ref, m1_ref)
    chunks = [(kc_ref, vc_ref, 0, n_ctx, 0)] + [
        (k_ref, v_ref, j * KC_ATTN, KC_ATTN, n_ctx + j * KC_ATTN) for j in range(n_chunks)]

    def lane_groups(x):
        return [x[:, t * LANES:(t + 1) * LANES] for t in range(x.shape[1] // LANES)]

    def segment(i, h_scores, h_values):
        lo = h_scores * HEAD_PAD
        q = q_ref[0, pl.ds(pl.multiple_of(i * tq, tq), tq), lo:lo + HEAD_PAD]
        s_out = s_refs[h_scores]
        m_run = acc = None
        if h_values is not None:
            s_in = s_refs[h_values]
            m_in = m_refs[h_values][...]
            lv = h_values * HEAD_PAD
        for kr, vr, r0, width, c0 in chunks:
            if h_values is not None:
                p = [jnp.exp2(g - m_in) for g in lane_groups(s_in[:, c0:c0 + width])]
                pv = _dot(jnp.concatenate(p, axis=1).astype(BF16), vr[0, r0:r0 + width, lv:lv + HEAD_PAD])
                acc = pv if acc is None else acc + pv
            s = _dot_nt(q, kr[0, r0:r0 + width, lo:lo + HEAD_PAD])
            s_out[:, c0:c0 + width] = s
            m_run = functools.reduce(jnp.maximum, lane_groups(s) if m_run is None else [m_run] + lane_groups(s))
        m_refs[h_scores][...] = jnp.broadcast_to(jnp.max(m_run, axis=-1, keepdims=True), (tq, LANES))
        return acc

    segment(0, 0, None)

    def body(i, carry):
        acc0 = segment(i, 1, 0)
        acc1 = segment(jnp.minimum(i + 1, n_tiles - 1), 0, 1)
        out0 = acc0 / acc0[:, V_HEAD:V_HEAD + 1]
        out1 = acc1 / acc1[:, 0:1]
        lane = lax.broadcasted_iota(jnp.int32, out0.shape, 1)
        o_ref[0, pl.ds(pl.multiple_of(i * tq, tq), tq), :] = jnp.where(lane < V_HEAD, out0, out1).astype(BF16)
        return carry

    lax.fori_loop(0, n_tiles, body, 0, unroll=4)


def _attn_call(q, k_ctx, k, v_ctx, v):
    bsz, n, hw = q.shape
    n_ctx = k_ctx.shape[1]
    tq = TQ_ATTN
    pair = 2 * HEAD_PAD
    return pl.pallas_call(
        _attn_kernel,
        grid=(bsz, MLA_HEADS // 2),
        in_specs=[
            pl.BlockSpec((1, n, pair), lambda b, h: (b, 0, h)),
            pl.BlockSpec((1, n_ctx, pair), lambda b, h: (b, 0, h)),
            pl.BlockSpec((1, n, pair), lambda b, h: (b, 0, h)),
            pl.BlockSpec((1, n_ctx, pair), lambda b, h: (b, 0, h)),
            pl.BlockSpec((1, n, pair), lambda b, h: (b, 0, h)),
        ],
        out_specs=pl.BlockSpec((1, n, 2 * V_HEAD), lambda b, h: (b, 0, h)),
        out_shape=jax.ShapeDtypeStruct((bsz, n, MLA_HEADS * V_HEAD), BF16),
        scratch_shapes=[pltpu.VMEM((tq, n_ctx + n), F32), pltpu.VMEM((tq, n_ctx + n), F32),
                        pltpu.VMEM((tq, LANES), F32), pltpu.VMEM((tq, LANES), F32)],
        compiler_params=_params(("parallel", "parallel")),
        name="attention",
    )(q, k_ctx, k, v_ctx, v)


def _ffn_tail(x1, mod_ref, g2n_ref, w1_ref, w3_ref, w2_ref, out_ref):
    h2 = _norm_mod(x1, g2n_ref[...], mod_ref[0, 3:4, :], mod_ref[0, 4:5, :]).astype(BF16)
    acc = None
    for c0, cw in FF_CHUNKS:
        a = _dot(h2, w1_ref[:, c0:c0 + cw])
        b = _dot(h2, w3_ref[:, c0:c0 + cw])
        part = _dot((_silu(a) * b).astype(BF16), w2_ref[c0:c0 + cw, :])
        acc = part if acc is None else acc + part
    out_ref[0] = x1 + mod_ref[0, 5:6, :] * acc


def _even_out_kernel(x_ref, o_ref, y_ref, mod_ref, g2n_ref, wo_ref, w1_ref, w3_ref, w2_ref, out_ref):
    hv = MLA_HEADS * V_HEAD
    mix = _dot(o_ref[0], wo_ref[0:hv, :]) + _dot(y_ref[0].astype(BF16), wo_ref[hv:, :])
    x1 = x_ref[0] + mod_ref[0, 2:3, :] * mix
    _ffn_tail(x1, mod_ref, g2n_ref, w1_ref, w3_ref, w2_ref, out_ref)


def _even_out_call(x, o, y, mod, g2n, wo, w1, w3, w2):
    bsz, n, d = x.shape
    tm = TM_OUT
    tok = lambda width: pl.BlockSpec((1, tm, width), lambda b, i: (b, i, 0))
    return pl.pallas_call(
        _even_out_kernel,
        grid=(bsz, n // tm),
        in_specs=[
            tok(d), tok(o.shape[-1]), tok(y.shape[-1]),
            pl.BlockSpec((1, 8, d), lambda b, i: (b, 0, 0)),
            _const_spec((1, d)),
            _const_spec(wo.shape), _const_spec(w1.shape), _const_spec(w3.shape), _const_spec(w2.shape),
        ],
        out_specs=tok(d),
        out_shape=jax.ShapeDtypeStruct((bsz, n, d), F32),
        compiler_params=_params(("parallel", "parallel")),
        name="even_out_ffn",
    )(x, o, y, mod, g2n, wo, w1, w3, w2)


CONV_RB = 64


SUBLANES = 8


def _conf_conv_block(cbuf, dw_ref, bias, r0, ls):
    rows = CONV_RB + SUBLANES
    acc = jnp.broadcast_to(bias, (CONV_RB, LANES))
    for r in range(SUBLANES):
        part = None
        for o in range(r, CONF_WIDTH + 1, SUBLANES):
            if o == 0:
                continue
            term = dw_ref[o - 1:o, ls] * cbuf[r0 + o - r:r0 + o - r + rows, ls]
            part = term if part is None else part + term
        acc = acc + part[r:r + CONV_RB]
    return acc


def _odd_kernel(xp_ref, x_ref, xn_ref, mod_ref, g1n_ref, win_ref, dw_ref, dwb_ref, lng_ref, lnb_ref, scw_ref,
                wout_ref, g2n_ref, w1_ref, w3_ref, w2_ref, out_ref, hb_ref, cbuf, sbuf, mbuf):
    tm = x_ref.shape[1]
    i = pl.program_id(1)
    last = pl.num_programs(1) - 1
    g = g1n_ref[...]
    sh = mod_ref[0, 0:1, :]
    sc = mod_ref[0, 1:2, :]
    x = x_ref[0]
    hb_ref[0:HALO, :] = _norm_mod(xp_ref[0], g, sh, sc).astype(BF16)
    hb_ref[HALO:HALO + tm, :] = _norm_mod(x, g, sh, sc).astype(BF16)
    hb_ref[HALO + tm:, :] = _norm_mod(xn_ref[0], g, sh, sc).astype(BF16)
    u = _dot(hb_ref[...], win_ref[...])

    conf = u[:, 0:CONF_W] * jax.nn.sigmoid(u[:, CONF_W:2 * CONF_W])
    cx = u[:, 2 * CONF_W + SC_W:2 * CONF_W + 2 * SC_W] * u[:, 2 * CONF_W + 2 * SC_W:]
    keep_lo = i > 0
    keep_hi = i < last
    cbuf[0:HALO, :] = jnp.where(keep_lo, conf[0:HALO], 0.0)
    cbuf[HALO:HALO + tm, :] = conf[HALO:HALO + tm]
    cbuf[HALO + tm:, :] = jnp.where(keep_hi, conf[HALO + tm:], 0.0)
    sbuf[0:HALO, :] = jnp.where(keep_lo, cx[0:HALO], 0.0)
    sbuf[HALO:HALO + tm, :] = cx[HALO:HALO + tm]
    sbuf[HALO + tm:, :] = jnp.where(keep_hi, cx[HALO + tm:], 0.0)
    sb = u[HALO:HALO + tm, 2 * CONF_W:2 * CONF_W + SC_W]

    assert HALO == (CONF_WIDTH - 1) // 2 + 1
    pad_s = (SC_WIDTH - 1) // 2
    for rb in range(tm // CONV_RB):
        r0 = rb * CONV_RB
        for lg in range(CONF_W // LANES):
            ls = slice(lg * LANES, (lg + 1) * LANES)
            acc = _conf_conv_block(cbuf, dw_ref, dwb_ref[:, ls], r0, ls)
            mu = jnp.mean(acc, axis=-1, keepdims=True)
            cen = acc - mu
            var = jnp.mean(cen * cen, axis=-1, keepdims=True)
            yn = cen * lax.rsqrt(var + EPS) * lng_ref[:, ls] + lnb_ref[:, ls]
            mbuf[r0:r0 + CONV_RB, ls] = _silu(yn).astype(BF16)

            acc = None
            for j in range(SC_WIDTH):
                o = HALO + r0 + j - pad_s
                term = scw_ref[j:j + 1, ls] * sbuf[o:o + CONV_RB, ls]
                acc = term if acc is None else acc + term
            mbuf[r0:r0 + CONV_RB, CONF_W + lg * LANES:CONF_W + (lg + 1) * LANES] = (
                sb[r0:r0 + CONV_RB, ls] * acc).astype(BF16)

    mix = _dot(mbuf[...], wout_ref[...])
    x1 = x + mod_ref[0, 2:3, :] * mix
    _ffn_tail(x1, mod_ref, g2n_ref, w1_ref, w3_ref, w2_ref, out_ref)


def _odd_call(x, mod, g1n, win, dw, dwb, lng, lnb, scw, wout, g2n, w1, w3, w2):
    bsz, n, d = x.shape
    tm = TM_ODD
    hb = tm // HALO
    nh = n // HALO
    return pl.pallas_call(
        _odd_kernel,
        grid=(bsz, n // tm),
        in_specs=[
            pl.BlockSpec((1, HALO, d), lambda b, i: (b, jnp.maximum(i * hb - 1, 0), 0)),
            pl.BlockSpec((1, tm, d), lambda b, i: (b, i, 0)),
            pl.BlockSpec((1, HALO, d), lambda b, i: (b, jnp.minimum((i + 1) * hb, nh - 1), 0)),
            pl.BlockSpec((1, 8, d), lambda b, i: (b, 0, 0)),
            _const_spec((1, d)),
            _const_spec(win.shape),
            _const_spec(dw.shape), _const_spec(dwb.shape), _const_spec(lng.shape), _const_spec(lnb.shape),
            _const_spec(scw.shape),
            _const_spec(wout.shape),
            _const_spec((1, d)),
            _const_spec(w1.shape), _const_spec(w3.shape), _const_spec(w2.shape),
        ],
        out_specs=pl.BlockSpec((1, tm, d), lambda b, i: (b, i, 0)),
        out_shape=jax.ShapeDtypeStruct((bsz, n, d), F32),
        scratch_shapes=[
            pltpu.VMEM((tm + 2 * HALO, d), BF16),
            pltpu.VMEM((tm + 2 * HALO, CONF_W), F32),
            pltpu.VMEM((tm + 2 * HALO, SC_W), F32),
            pltpu.VMEM((tm, CONF_W + SC_W), BF16),
        ],
        compiler_params=_params(("parallel", "arbitrary")),
        name="odd_layer",
    )(x, x, x, mod, g1n, win, dw, dwb, lng, lnb, scw, wout, g2n, w1, w3, w2)


def _dft_tables():
    r = FFT_R
    gw = FNET_GROUP_W
    idx = np.arange(gw)
    ang = 2.0 * np.pi * ((idx[:, None] * idx[None, :]) % gw) / gw
    cs = np.concatenate([np.cos(ang), -np.sin(ang)], axis=1) / np.sqrt(gw)
    a = np.arange(r)
    nb = DFT_NB
    eye = np.eye(nb)
    ang1 = 2.0 * np.pi * ((a[:, None] * a[None, :]) % r) / r
    fr, fi = np.cos(ang1) / 8.0, -np.sin(ang1) / 8.0
    a_re = np.concatenate([np.kron(fr, eye), np.kron(fi, eye)], axis=0)
    a_im = np.concatenate([np.kron(-fi, eye), np.kron(fr, eye)], axis=0)
    n = r * r
    c = a[:, None, None]
    dd = a[None, :, None]
    bb = a[None, None, :]
    ang2 = 2.0 * np.pi * ((bb * (c + r * dd)) % n) / n
    gr, gi_neg = np.cos(ang2) / 8.0, np.sin(ang2) / 8.0
    b_re = np.zeros((r // nb, r, nb, nb, r))
    b_im = np.zeros((r // nb, r, nb, nb, r))
    for ci in range(nb):
        b_re[:, :, ci, ci, :] = gr[ci::nb]
        b_im[:, :, ci, ci, :] = gi_neg[ci::nb]
    b_re = b_re.reshape(r // nb, r * nb, nb * r)
    b_im = b_im.reshape(r // nb, r * nb, nb * r)
    named = (("cs", cs), ("a_re", a_re), ("a_im", a_im), ("b_re", b_re), ("b_im", b_im))
    return {name: jnp.asarray(t, F32).astype(BF16) for name, t in named}


def _rope_tables(n):
    rows = n // GRID_W
    row = jnp.broadcast_to(jnp.arange(rows, dtype=F32)[:, None], (rows, GRID_W)).reshape(n)
    col = jnp.broadcast_to(jnp.arange(GRID_W, dtype=F32)[None, :], (rows, GRID_W)).reshape(n)
    per_axis = QK_ROPE // 4
    inv_freq = ROPE_BASE ** (-jnp.arange(per_axis, dtype=F32) / per_axis)
    ang = jnp.concatenate([row[:, None] * inv_freq, col[:, None] * inv_freq], axis=-1)
    cos, sin = jnp.cos(ang), jnp.sin(ang)
    tail = LANES - QK_HEAD
    rc = jnp.concatenate([jnp.ones((n, QK_NOPE), F32), cos, cos, jnp.ones((n, tail), F32)], axis=-1)
    rs = jnp.concatenate([jnp.zeros((n, QK_NOPE), F32), -sin, sin, jnp.zeros((n, tail), F32)], axis=-1)
    return rc, rs


def _swap_rope_halves(t):
    r1 = t[..., QK_NOPE:QK_NOPE + ROPE_HALF]
    r2 = t[..., QK_NOPE + ROPE_HALF:]
    return jnp.concatenate([jnp.zeros_like(t[..., :QK_NOPE]), r2, r1], axis=-1)


def _pad_heads(w, width):
    r = w.shape[0]
    return jnp.pad(w, ((0, 0), (0, 0), (0, HEAD_PAD - width))).reshape(r, MLA_HEADS * HEAD_PAD)


def _pad_gain(g):
    return jnp.pad(g, (0, HEAD_PAD - QK_HEAD)).reshape(1, HEAD_PAD)


def _pad_values(w):
    r = w.shape[0]
    pairs = w.reshape(r, MLA_HEADS // 2, 2, V_HEAD)
    zero = jnp.zeros_like(pairs[:, :, 0])
    even = jnp.concatenate([pairs[:, :, 0], zero], axis=-1)
    odd = jnp.concatenate([zero, pairs[:, :, 1]], axis=-1)
    return jnp.stack([even, odd], axis=2).reshape(r, MLA_HEADS * HEAD_PAD)


def _value_ones_row():
    row = np.zeros((MLA_HEADS, HEAD_PAD), np.float32)
    row[0::2, V_HEAD] = 1.0
    row[1::2, 0] = 1.0
    return jnp.asarray(row.reshape(1, MLA_HEADS * HEAD_PAD))


def kernel(x, c, ctx, c_ctx, ada_w, ada_b, norm1_g, norm2_g, ffn_w1, ffn_w3, ffn_w2, a_w_in, a_q_ln_g, a_kv_ln_g, a_w_uq, a_w_uk, a_w_uv, a_q_norm_g, a_k_norm_g, a_w_out, b_w_in, b_conf_dw, b_conf_dw_b, b_conf_ln_g, b_conf_ln_b, b_sc_dw, b_w_out):
    bsz, n, d = x.shape
    depth = ada_w.shape[0]
    assert depth == 2 and d == D_MODEL and n == FFT_R * FFT_R and ctx.shape[1] == CTX_LEN

    rows = 16
    cvec = jnp.concatenate([c, c_ctx[None, :], jnp.zeros((rows - bsz - 1, d), F32)], axis=0)
    ada = _ada_call(cvec, ada_w, ada_b).reshape(depth, rows, 6, d)
    unit = jnp.array([0.0, 1.0, 0.0, 0.0, 1.0, 0.0], F32)[None, None, :, None]
    mods = jnp.pad(ada + unit, ((0, 0), (0, 0), (0, 2), (0, 0)))

    tables = _dft_tables()
    tables["rc"], tables["rs"] = _rope_tables(n)
    tables["ones"] = jnp.ones((LANES, LANES), BF16)

    w_in = a_w_in[0]
    kr_cols = jnp.pad(w_in[:, Q_LORA + KV_LORA:EVEN_KV_END], ((0, 0), (QK_NOPE, LANES - QK_HEAD)))
    w = {
        "wq": jnp.concatenate([w_in[:, :Q_LORA], kr_cols], axis=1).astype(BF16),
        "wkv": w_in[:, Q_LORA:Q_LORA + KV_LORA].astype(BF16),
        "wf": w_in[:, EVEN_KV_END:].astype(BF16),
        "qlg": a_q_ln_g[0].reshape(1, Q_LORA),
        "kvlg": a_kv_ln_g[0].reshape(1, KV_LORA),
        "wuq": _pad_heads(a_w_uq[0], QK_HEAD).astype(BF16),
        "wuqs": _pad_heads(_swap_rope_halves(a_w_uq[0]), QK_HEAD).astype(BF16),
        "wuk": _pad_heads(a_w_uk[0], QK_NOPE).astype(BF16),
        "wuv": _pad_values(a_w_uv[0]).astype(BF16),
        "vone": _value_ones_row(),
        "gq": _pad_gain(a_q_norm_g[0]),
        "gqs": _pad_gain(_swap_rope_halves(a_q_norm_g[0])),
        "gk": _pad_gain(a_k_norm_g[0]),
    }
    g1 = norm1_g[0].reshape(1, d)
    q, k, v, z = _even_in_call(x, mods[0, :bsz], g1, w, tables)
    k_ctx, v_ctx = _ctx_in_call(ctx, mods[0, bsz:bsz + 1], g1, w, tables)
    y = _seq_dft_call(z, tables)
    o = _attn_call(q, k_ctx, k, v_ctx, v)
    x = _even_out_call(x, o, y, mods[0, :bsz], norm2_g[0].reshape(1, d), a_w_out[0].astype(BF16),
                       ffn_w1[0].astype(BF16), ffn_w3[0].astype(BF16), ffn_w2[0].astype(BF16))

    x = _odd_call(x, mods[1, :bsz], norm1_g[1].reshape(1, d), b_w_in[0].astype(BF16),
                  b_conf_dw[0], b_conf_dw_b[0].reshape(1, CONF_W), b_conf_ln_g[0].reshape(1, CONF_W),
                  b_conf_ln_b[0].reshape(1, CONF_W), b_sc_dw[0], b_w_out[0].astype(BF16),
                  norm2_g[1].reshape(1, d), ffn_w1[1].astype(BF16), ffn_w3[1].astype(BF16),
                  ffn_w2[1].astype(BF16))
    return x
```

```python
import functools
import math

import numpy as np
import jax
import jax.numpy as jnp
from jax import lax
from jax.experimental import pallas as pl
from jax.experimental.pallas import tpu as pltpu

F32 = jnp.float32
BF16 = jnp.bfloat16

D_MODEL = 1024
CTX_LEN = 256
GRID_W = 64
EPS = 1e-6
MLA_HEADS = 8
QK_NOPE = 64
QK_ROPE = 32
QK_HEAD = QK_NOPE + QK_ROPE
V_HEAD = 64
Q_LORA = 384
KV_LORA = 256
ROPE_BASE = 10000.0
FNET_GROUPS = 4
FNET_GROUP_W = 128
FNET_W = FNET_GROUPS * FNET_GROUP_W
CONF_GROUPS = 4
CONF_W = 512
CONF_WIDTH = 31
SC_W = 512
SC_WIDTH = 3
D_FF = 2816
EVEN_KV_END = Q_LORA + KV_LORA + QK_ROPE

LANES = 128
HEAD_PAD = LANES
ROPE_HALF = QK_ROPE // 2
FFT_R = 64
HALO = 16
VMEM_LIMIT = 56 * 1024 * 1024

TM_EVEN_IN = 512
TQ_ATTN = 512
KC_ATTN = 512
TM_OUT = 512
TM_ODD = 512
FF_CHUNKS = tuple((c, min(512, D_FF - c)) for c in range(0, D_FF, 512))

Q_SCALE = QK_HEAD ** -0.5 * math.log2(math.e)


def _dot(a, b):
    return jnp.dot(a, b, preferred_element_type=F32)


def _dot_nt(a, b):
    return lax.dot_general(a, b, (((1,), (1,)), ((), ())), preferred_element_type=F32)


def _inv_rms(x, n):
    return lax.rsqrt(jnp.sum(x * x, axis=-1, keepdims=True) * (1.0 / n) + EPS)


def _norm_mod(x, g, shift, scale1p):
    return (x * _inv_rms(x, x.shape[-1]) * g) * scale1p + shift


def _silu(a):
    return a * jax.nn.sigmoid(a)


def _const_spec(shape):
    nd = len(shape)
    return pl.BlockSpec(shape, lambda *_: (0,) * nd, pipeline_mode=pl.Buffered(1))


def _params(sem, flags=None):
    return pltpu.CompilerParams(dimension_semantics=sem, vmem_limit_bytes=VMEM_LIMIT, flags=flags)


def _ada_kernel(c_ref, w_ref, b_ref, o_ref):
    s = _silu(c_ref[...]).astype(BF16)
    o_ref[0] = _dot(s, w_ref[0].astype(BF16)) + b_ref[0]


def _ada_call(cvec, ada_w, ada_b):
    depth, d, n6 = ada_w.shape
    rows = cvec.shape[0]
    tn = 1536
    return pl.pallas_call(
        _ada_kernel,
        grid=(depth, n6 // tn),
        in_specs=[
            pl.BlockSpec((rows, d), lambda l, j: (0, 0)),
            pl.BlockSpec((1, d, tn), lambda l, j: (l, 0, j)),
            pl.BlockSpec((1, 1, tn), lambda l, j: (l, 0, j)),
        ],
        out_specs=pl.BlockSpec((1, rows, tn), lambda l, j: (l, 0, j)),
        out_shape=jax.ShapeDtypeStruct((depth, rows, n6), F32),
        compiler_params=_params(("arbitrary", "arbitrary")),
        name="ada_mod",
    )(cvec, ada_w, ada_b.reshape(depth, 1, n6))


def _head_inv_rms(t, ones):
    ss = _dot((t * t).astype(BF16), ones)
    return lax.rsqrt(ss * (1.0 / QK_HEAD) + EPS)


def _keys_values(hb, krb, rope, wkv_ref, kvlg_ref, wuk_ref, wuv_ref, vone_ref, gk_ref, ones_ref, k_ref, v_ref):
    ckv = _dot(hb, wkv_ref[...])
    ckvn = (ckv * _inv_rms(ckv, KV_LORA) * kvlg_ref[...]).astype(BF16)
    kf = _dot(ckvn, wuk_ref[...])
    v_ref[0] = (_dot(ckvn, wuv_ref[...]) + vone_ref[...]).astype(BF16)
    gk = gk_ref[...]
    krg = krb * gk
    if rope is not None:
        rc, rs = rope
        lane = lax.broadcasted_iota(jnp.int32, krg.shape, 1)
        partner = jnp.where(lane < QK_NOPE + ROPE_HALF, pltpu.roll(krg, LANES - ROPE_HALF, 1),
                            pltpu.roll(krg, ROPE_HALF, 1))
        krg = krg * rc + partner * rs
    ones = ones_ref[...]
    for h in range(MLA_HEADS):
        kn = kf[:, h * HEAD_PAD:(h + 1) * HEAD_PAD]
        r = _head_inv_rms(kn + krb, ones)
        k_ref[0, :, h * HEAD_PAD:(h + 1) * HEAD_PAD] = ((kn * gk + krg) * r).astype(BF16)


def _even_in_kernel(x_ref, mod_ref, g_ref, wq_ref, wkv_ref, wf_ref, qlg_ref, kvlg_ref, wuq_ref, wuqs_ref, wuk_ref,
                    wuv_ref, vone_ref, gq_ref, gqs_ref, gk_ref, rc_ref, rs_ref, cs_ref, ones_ref,
                    q_ref, k_ref, v_ref, z_ref):
    hb = _norm_mod(x_ref[0], g_ref[...], mod_ref[0, 0:1, :], mod_ref[0, 1:2, :]).astype(BF16)
    rc, rs = rc_ref[...], rs_ref[...]
    uq = _dot(hb, wq_ref[...])
    _keys_values(hb, uq[:, Q_LORA:], (rc, rs), wkv_ref, kvlg_ref, wuk_ref, wuv_ref, vone_ref, gk_ref, ones_ref,
                 k_ref, v_ref)

    cq = uq[:, :Q_LORA]
    cqn = (cq * _inv_rms(cq, Q_LORA) * qlg_ref[...]).astype(BF16)
    qf = _dot(cqn, wuq_ref[...])
    qs = _dot(cqn, wuqs_ref[...])
    gq = gq_ref[...] * Q_SCALE
    gqs = gqs_ref[...] * Q_SCALE
    ones = ones_ref[...]
    for h in range(MLA_HEADS):
        sl = slice(h * HEAD_PAD, (h + 1) * HEAD_PAD)
        qh = qf[:, sl]
        r = _head_inv_rms(qh, ones)
        q_ref[0, :, sl] = ((qh * gq * rc + qs[:, sl] * gqs * rs) * r).astype(BF16)

    uf = _dot(hb, wf_ref[...]).astype(BF16)
    for g in range(FNET_GROUPS):
        zz = _dot(uf[:, g * LANES:(g + 1) * LANES], cs_ref[...])
        z_ref[0, :, g * LANES:(g + 1) * LANES] = zz[:, :LANES]
        z_ref[0, :, FNET_W + g * LANES:FNET_W + (g + 1) * LANES] = zz[:, LANES:]


def _ctx_in_kernel(x_ref, mod_ref, g_ref, wq_ref, wkv_ref, kvlg_ref, wuk_ref, wuv_ref, vone_ref, gk_ref, ones_ref,
                   k_ref, v_ref):
    hb = _norm_mod(x_ref[0], g_ref[...], mod_ref[0, 0:1, :], mod_ref[0, 1:2, :]).astype(BF16)
    krb = _dot(hb, wq_ref[:, Q_LORA:])
    _keys_values(hb, krb, None, wkv_ref, kvlg_ref, wuk_ref, wuv_ref, vone_ref, gk_ref, ones_ref, k_ref, v_ref)


def _even_in_call(x, mod, g, w, tables):
    bsz, n, d = x.shape
    tm = TM_EVEN_IN
    hw = MLA_HEADS * HEAD_PAD
    tok = lambda width: pl.BlockSpec((1, tm, width), lambda b, i: (b, i, 0))
    tab = pl.BlockSpec((tm, LANES), lambda b, i: (i, 0))
    return pl.pallas_call(
        _even_in_kernel,
        grid=(bsz, n // tm),
        in_specs=[
            tok(d),
            pl.BlockSpec((1, 8, d), lambda b, i: (b, 0, 0)),
            _const_spec((1, d)),
            _const_spec(w["wq"].shape), _const_spec(w["wkv"].shape), _const_spec(w["wf"].shape),
            _const_spec((1, Q_LORA)), _const_spec((1, KV_LORA)),
            _const_spec(w["wuq"].shape), _const_spec(w["wuqs"].shape), _const_spec(w["wuk"].shape),
            _const_spec(w["wuv"].shape), _const_spec((1, hw)),
            _const_spec((1, LANES)), _const_spec((1, LANES)), _const_spec((1, LANES)),
            tab, tab,
            _const_spec(tables["cs"].shape), _const_spec(tables["ones"].shape),
        ],
        out_specs=[tok(hw), tok(hw), tok(hw), tok(2 * FNET_W)],
        out_shape=[
            jax.ShapeDtypeStruct((bsz, n, hw), BF16),
            jax.ShapeDtypeStruct((bsz, n, hw), BF16),
            jax.ShapeDtypeStruct((bsz, n, hw), BF16),
            jax.ShapeDtypeStruct((bsz, n, 2 * FNET_W), F32),
        ],
        compiler_params=_params(("parallel", "parallel")),
        name="even_in",
    )(x, mod, g, w["wq"], w["wkv"], w["wf"], w["qlg"], w["kvlg"], w["wuq"], w["wuqs"], w["wuk"], w["wuv"],
      w["vone"], w["gq"], w["gqs"], w["gk"], tables["rc"], tables["rs"], tables["cs"], tables["ones"])


def _ctx_in_call(ctx, mod, g, w, tables):
    bsz, n, d = ctx.shape
    hw = MLA_HEADS * HEAD_PAD
    tok = lambda width: pl.BlockSpec((1, n, width), lambda b: (b, 0, 0))
    return pl.pallas_call(
        _ctx_in_kernel,
        grid=(bsz,),
        in_specs=[
            tok(d),
            pl.BlockSpec((1, 8, d), lambda b: (0, 0, 0)),
            _const_spec((1, d)),
            _const_spec(w["wq"].shape), _const_spec(w["wkv"].shape),
            _const_spec((1, KV_LORA)),
            _const_spec(w["wuk"].shape), _const_spec(w["wuv"].shape), _const_spec((1, hw)),
            _const_spec((1, LANES)), _const_spec(tables["ones"].shape),
        ],
        out_specs=[tok(hw), tok(hw)],
        out_shape=[
            jax.ShapeDtypeStruct((bsz, n, hw), BF16),
            jax.ShapeDtypeStruct((bsz, n, hw), BF16),
        ],
        compiler_params=_params(("parallel",)),
        name="ctx_in",
    )(ctx, mod, g, w["wq"], w["wkv"], w["kvlg"], w["wuk"], w["wuv"], w["vone"], w["gk"], tables["ones"])


DFT_NB = 8


def _seq_dft_kernel(z_ref, are_ref, aim_ref, bre_ref, bim_ref, y_ref, t_scr):
    r = FFT_R
    half = r // DFT_NB
    s = pl.program_id(1)

    @pl.when(s < half)
    def _():
        z = z_ref[0].reshape(r * DFT_NB, 2 * FNET_W).astype(BF16)
        t = _dot(are_ref[...], z[:, :FNET_W]) + _dot(aim_ref[...], z[:, FNET_W:])
        t_scr[:, pl.ds(pl.multiple_of(s * DFT_NB, DFT_NB), DFT_NB), :] = t.reshape(2 * r, DFT_NB, FNET_W)

    @pl.when(s >= half)
    def _():
        j = s - half
        c0 = pl.multiple_of(j * DFT_NB, DFT_NB)
        xr = t_scr[pl.ds(c0, DFT_NB)].reshape(DFT_NB * r, FNET_W).astype(BF16)
        xi = t_scr[pl.ds(r + c0, DFT_NB)].reshape(DFT_NB * r, FNET_W).astype(BF16)
        y = _dot(bre_ref[j], xr) + _dot(bim_ref[j], xi)
        y_ref[0] = y.reshape(y_ref.shape[1:])


def _seq_dft_call(z, tables):
    bsz, n, w2 = z.shape
    r = FFT_R
    nb = DFT_NB
    half = r // nb
    y = pl.pallas_call(
        _seq_dft_kernel,
        grid=(bsz, 2 * half),
        in_specs=[
            pl.BlockSpec((1, r, nb, w2), lambda b, s: (b, 0, jnp.minimum(s, half - 1), 0)),
            _const_spec(tables["a_re"].shape), _const_spec(tables["a_im"].shape),
            _const_spec(tables["b_re"].shape), _const_spec(tables["b_im"].shape),
        ],
        out_specs=pl.BlockSpec((1, r, nb, FNET_W), lambda b, s: (b, 0, jnp.maximum(s - half, 0), 0)),
        out_shape=jax.ShapeDtypeStruct((bsz, r, r, FNET_W), F32),
        scratch_shapes=[pltpu.VMEM((2 * r, r, FNET_W), F32)],
        compiler_params=_params(("parallel", "arbitrary")),
        name="seq_dft",
    )(z.reshape(bsz, r, r, w2), tables["a_re"], tables["a_im"], tables["b_re"], tables["b_im"])
    return y.reshape(bsz, n, FNET_W)


def _attn_kernel(q_ref, kc_ref, k_ref, vc_ref, v_ref, o_ref, s0_ref, s1_ref, m0_ref, m1_ref):
    n_ctx = kc_ref.shape[1]
    n_chunks = k_ref.shape[1] // KC_ATTN
    tq = TQ_ATTN
    n_tiles = q_ref.shape[1] // tq
    s_refs = (s0_ref, s1_ref)
    m_refs = (m0_ref, m1_ref)
    chunks = [(kc_ref, vc_ref, 0, n_ctx, 0)] + [
        (k_ref, v_ref, j * KC_ATTN, KC_ATTN, n_ctx + j * KC_ATTN) for j in range(n_chunks)]

    def lane_groups(x):
        return [x[:, t * LANES:(t + 1) * LANES] for t in range(x.shape[1] // LANES)]

    def segment(i, h_scores, h_values):
        lo = h_scores * HEAD_PAD
        q = q_ref[0, pl.ds(pl.multiple_of(i * tq, tq), tq), lo:lo + HEAD_PAD]
        s_out = s_refs[h_scores]
        m_run = acc = None
        if h_values is not None:
            s_in = s_refs[h_values]
            m_in = m_refs[h_values][...]
            lv = h_values * HEAD_PAD
        for kr, vr, r0, width, c0 in chunks:
            if h_values is not None:
                p = [jnp.exp2(g - m_in) for g in lane_groups(s_in[:, c0:c0 + width])]
                pv = _dot(jnp.concatenate(p, axis=1).astype(BF16), vr[0, r0:r0 + width, lv:lv + HEAD_PAD])
                acc = pv if acc is None else acc + pv
            s = _dot_nt(q, kr[0, r0:r0 + width, lo:lo + HEAD_PAD])
            s_out[:, c0:c0 + width] = s
            m_run = functools.reduce(jnp.maximum, lane_groups(s) if m_run is None else [m_run] + lane_groups(s))
        m_refs[h_scores][...] = jnp.broadcast_to(jnp.max(m_run, axis=-1, keepdims=True), (tq, LANES))
        return acc

    segment(0, 0, None)

    def body(i, carry):
        acc0 = segment(i, 1, 0)
        acc1 = segment(jnp.minimum(i + 1, n_tiles - 1), 0, 1)
        out0 = acc0 / acc0[:, V_HEAD:V_HEAD + 1]
        out1 = acc1 / acc1[:, 0:1]
        lane = lax.broadcasted_iota(jnp.int32, out0.shape, 1)
        o_ref[0, pl.ds(pl.multiple_of(i * tq, tq), tq), :] = jnp.where(lane < V_HEAD, out0, out1).astype(BF16)
        return carry

    lax.fori_loop(0, n_tiles, body, 0, unroll=4)


def _attn_call(q, k_ctx, k, v_ctx, v):
    bsz, n, hw = q.shape
    n_ctx = k_ctx.shape[1]
    tq = TQ_ATTN
    pair = 2 * HEAD_PAD
    return pl.pallas_call(
        _attn_kernel,
        grid=(bsz, MLA_HEADS // 2),
        in_specs=[
            pl.BlockSpec((1, n, pair), lambda b, h: (b, 0, h)),
            pl.BlockSpec((1, n_ctx, pair), lambda b, h: (b, 0, h)),
            pl.BlockSpec((1, n, pair), lambda b, h: (b, 0, h)),
            pl.BlockSpec((1, n_ctx, pair), lambda b, h: (b, 0, h)),
            pl.BlockSpec((1, n, pair), lambda b, h: (b, 0, h)),
        ],
        out_specs=pl.BlockSpec((1, n, 2 * V_HEAD), lambda b, h: (b, 0, h)),
        out_shape=jax.ShapeDtypeStruct((bsz, n, MLA_HEADS * V_HEAD), BF16),
        scratch_shapes=[pltpu.VMEM((tq, n_ctx + n), F32), pltpu.VMEM((tq, n_ctx + n), F32),
                        pltpu.VMEM((tq, LANES), F32), pltpu.VMEM((tq, LANES), F32)],
        compiler_params=_params(("parallel", "parallel")),
        name="attention",
    )(q, k_ctx, k, v_ctx, v)


def _ffn_tail(x1, mod_ref, g2n_ref, w1_ref, w3_ref, w2_ref, out_ref):
    h2 = _norm_mod(x1, g2n_ref[...], mod_ref[0, 3:4, :], mod_ref[0, 4:5, :]).astype(BF16)
    acc = None
    for c0, cw in FF_CHUNKS:
        a = _dot(h2, w1_ref[:, c0:c0 + cw])
        b = _dot(h2, w3_ref[:, c0:c0 + cw])
        part = _dot((_silu(a) * b).astype(BF16), w2_ref[c0:c0 + cw, :])
        acc = part if acc is None else acc + part
    out_ref[0] = x1 + mod_ref[0, 5:6, :] * acc


def _even_out_kernel(x_ref, o_ref, y_ref, mod_ref, g2n_ref, wo_ref, w1_ref, w3_ref, w2_ref, out_ref):
    hv = MLA_HEADS * V_HEAD
    mix = _dot(o_ref[0], wo_ref[0:hv, :]) + _dot(y_ref[0].astype(BF16), wo_ref[hv:, :])
    x1 = x_ref[0] + mod_ref[0, 2:3, :] * mix
    _ffn_tail(x1, mod_ref, g2n_ref, w1_ref, w3_ref, w2_ref, out_ref)


def _even_out_call(x, o, y, mod, g2n, wo, w1, w3, w2):
    bsz, n, d = x.shape
    tm = TM_OUT
    tok = lambda width: pl.BlockSpec((1, tm, width), lambda b, i: (b, i, 0))
    return pl.pallas_call(
        _even_out_kernel,
        grid=(bsz, n // tm),
        in_specs=[
            tok(d), tok(o.shape[-1]), tok(y.shape[-1]),
            pl.BlockSpec((1, 8, d), lambda b, i: (b, 0, 0)),
            _const_spec((1, d)),
            _const_spec(wo.shape), _const_spec(w1.shape), _const_spec(w3.shape), _const_spec(w2.shape),
        ],
        out_specs=tok(d),
        out_shape=jax.ShapeDtypeStruct((bsz, n, d), F32),
        compiler_params=_params(("parallel", "parallel")),
        name="even_out_ffn",
    )(x, o, y, mod, g2n, wo, w1, w3, w2)


CONV_RB = 64


SUBLANES = 8


def _conf_conv_block(cbuf, dw_ref, bias, r0, ls):
    rows = CONV_RB + SUBLANES
    acc = jnp.broadcast_to(bias, (CONV_RB, LANES))
    for r in range(SUBLANES):
        part = None
        for o in range(r, CONF_WIDTH + 1, SUBLANES):
            if o == 0:
                continue
            term = dw_ref[o - 1:o, ls] * cbuf[r0 + o - r:r0 + o - r + rows, ls]
            part = term if part is None else part + term
        acc = acc + part[r:r + CONV_RB]
    return acc


def _zero_row(val_row, zero_ref):
    bits = lax.bitcast_convert_type(val_row, jnp.int32) & zero_ref[...]
    return lax.bitcast_convert_type(bits, F32)


def _odd_kernel(xp_ref, x_ref, xn_ref, xlag_ref, mod_ref, modlag_ref, zero_ref, g1n_ref, win_ref, dw_ref, dwb_ref,
                lng_ref, lnb_ref, scw_ref, wout_ref, g2n_ref, w1_ref, w3_ref, w2_ref, out_ref,
                hb_ref, cbuf, sbuf, bbuf, mbuf, acc_ref, *, tiles_per_seq):
    tm = x_ref.shape[1]
    t = pl.program_id(0)

    @pl.when(t == 0)
    def _():
        mbuf[...] = jnp.zeros(mbuf.shape, mbuf.dtype)

    mix = _dot(mbuf[...], wout_ref[...])
    x1 = xlag_ref[0] + modlag_ref[0, 2:3, :] * mix
    h2 = _norm_mod(x1, g2n_ref[...], modlag_ref[0, 3:4, :], modlag_ref[0, 4:5, :]).astype(BF16)

    i = lax.rem(jnp.minimum(t, pl.num_programs(0) - 2), tiles_per_seq)
    last = tiles_per_seq - 1
    g = g1n_ref[...]
    sh = mod_ref[0, 0:1, :]
    sc = mod_ref[0, 1:2, :]
    hb_ref[0:HALO, :] = _norm_mod(xp_ref[0], g, sh, sc).astype(BF16)
    hb_ref[HALO:HALO + tm, :] = _norm_mod(x_ref[0], g, sh, sc).astype(BF16)
    hb_ref[HALO + tm:, :] = _norm_mod(xn_ref[0], g, sh, sc).astype(BF16)
    u = _dot(hb_ref[...], win_ref[...])

    conf = u[:, 0:CONF_W] * jax.nn.sigmoid(u[:, CONF_W:2 * CONF_W])
    cx = u[:, 2 * CONF_W + SC_W:2 * CONF_W + 2 * SC_W] * u[:, 2 * CONF_W + 2 * SC_W:]
    keep_lo = i > 0
    keep_hi = i < last
    cbuf[0:HALO, :] = jnp.where(keep_lo, conf[0:HALO], 0.0)
    cbuf[HALO:HALO + tm, :] = conf[HALO:HALO + tm]
    cbuf[HALO + tm:, :] = jnp.where(keep_hi, conf[HALO + tm:], 0.0)
    sbuf[0:HALO, :] = jnp.where(keep_lo, cx[0:HALO], 0.0)
    sbuf[HALO:HALO + tm, :] = cx[HALO:HALO + tm]
    sbuf[HALO + tm:, :] = jnp.where(keep_hi, cx[HALO + tm:], 0.0)
    bbuf[...] = u[HALO:HALO + tm, 2 * CONF_W:2 * CONF_W + SC_W]

    assert HALO == (CONF_WIDTH - 1) // 2 + 1
    pad_s = (SC_WIDTH - 1) // 2

    def conv_block(r0, lg, order_after):
        ls = slice(lg * LANES, (lg + 1) * LANES)
        bias = dwb_ref[:, ls]
        if order_after is not None:
            bias = bias + order_after
        acc = _conf_conv_block(cbuf, dw_ref, bias, r0, ls)
        mu = jnp.mean(acc, axis=-1, keepdims=True)
        cen = acc - mu
        var = jnp.mean(cen * cen, axis=-1, keepdims=True)
        yn = cen * lax.rsqrt(var + EPS) * lng_ref[:, ls] + lnb_ref[:, ls]
        mbuf[r0:r0 + CONV_RB, ls] = _silu(yn).astype(BF16)

        acc = None
        for j in range(SC_WIDTH):
            o = HALO + r0 + j - pad_s
            term = scw_ref[j:j + 1, ls] * sbuf[o:o + CONV_RB, ls]
            acc = term if acc is None else acc + term
        mbuf[r0:r0 + CONV_RB, CONF_W + lg * LANES:CONF_W + (lg + 1) * LANES] = (
            bbuf[r0:r0 + CONV_RB, ls] * acc).astype(BF16)
        return yn[0:1, :]

    blocks = [(r0, lg) for r0 in range(0, tm, CONV_RB) for lg in range(CONF_W // LANES)]
    n_chunks = len(FF_CHUNKS)
    after_chunk = None
    after_conv = None
    for k, (c0, cw) in enumerate(FF_CHUNKS):
        a = _dot(h2, w1_ref[:, c0:c0 + cw])
        b = _dot(h2, w3_ref[:, c0:c0 + cw])
        part = _dot((_silu(a) * b).astype(BF16), w2_ref[c0:c0 + cw, :])
        if k == 0:
            acc_ref[...] = part
        else:
            acc_ref[...] += part
        if after_conv is not None:
            acc_ref[0:1, 0:LANES] += after_conv
        tail = None
        for r0, lg in blocks[k * len(blocks) // n_chunks:(k + 1) * len(blocks) // n_chunks]:
            tail = conv_block(r0, lg, after_chunk)
        after_conv = _zero_row(tail, zero_ref)
        after_chunk = _zero_row(part[0:1, 0:LANES], zero_ref)
    out_ref[0] = x1 + modlag_ref[0, 5:6, :] * acc_ref[...]


def _odd_call(x, mod, g1n, win, dw, dwb, lng, lnb, scw, wout, g2n, w1, w3, w2):
    bsz, n, d = x.shape
    tm = TM_ODD
    hb = tm // HALO
    nh = n // HALO
    nt = n // tm
    total = bsz * nt

    def cur(t):
        u = jnp.minimum(t, total - 1)
        return u // nt, u % nt

    def lag(t):
        u = jnp.maximum(t - 1, 0)
        return u // nt, u % nt

    def halo_lo(t):
        b, i = cur(t)
        return b, jnp.maximum(i * hb - 1, 0), 0

    def halo_hi(t):
        b, i = cur(t)
        return b, jnp.minimum((i + 1) * hb, nh - 1), 0

    return pl.pallas_call(
        functools.partial(_odd_kernel, tiles_per_seq=nt),
        grid=(total + 1,),
        in_specs=[
            pl.BlockSpec((1, HALO, d), halo_lo),
            pl.BlockSpec((1, tm, d), lambda t: (*cur(t), 0)),
            pl.BlockSpec((1, HALO, d), halo_hi),
            pl.BlockSpec((1, tm, d), lambda t: (*lag(t), 0)),
            pl.BlockSpec((1, 8, d), lambda t: (cur(t)[0], 0, 0)),
            pl.BlockSpec((1, 8, d), lambda t: (lag(t)[0], 0, 0)),
            _const_spec((1, LANES)),
            _const_spec((1, d)),
            _const_spec(win.shape),
            _const_spec(dw.shape), _const_spec(dwb.shape), _const_spec(lng.shape), _const_spec(lnb.shape),
            _const_spec(scw.shape),
            _const_spec(wout.shape),
            _const_spec((1, d)),
            _const_spec(w1.shape), _const_spec(w3.shape), _const_spec(w2.shape),
        ],
        out_specs=pl.BlockSpec((1, tm, d), lambda t: (*lag(t), 0)),
        out_shape=jax.ShapeDtypeStruct((bsz, n, d), F32),
        scratch_shapes=[
            pltpu.VMEM((tm + 2 * HALO, d), BF16),
            pltpu.VMEM((tm + 2 * HALO, CONF_W), F32),
            pltpu.VMEM((tm + 2 * HALO, SC_W), F32),
            pltpu.VMEM((tm, SC_W), F32),
            pltpu.VMEM((tm, CONF_W + SC_W), BF16),
            pltpu.VMEM((tm, d), F32),
        ],
        compiler_params=_params(("arbitrary",)),
        name="odd_layer",
    )(x, x, x, x, mod, mod, jnp.zeros((1, LANES), jnp.int32), g1n, win, dw, dwb, lng, lnb, scw, wout, g2n,
      w1, w3, w2)


def _dft_tables():
    r = FFT_R
    gw = FNET_GROUP_W
    idx = np.arange(gw)
    ang = 2.0 * np.pi * ((idx[:, None] * idx[None, :]) % gw) / gw
    cs = np.concatenate([np.cos(ang), -np.sin(ang)], axis=1) / np.sqrt(gw)
    a = np.arange(r)
    nb = DFT_NB
    eye = np.eye(nb)
    ang1 = 2.0 * np.pi * ((a[:, None] * a[None, :]) % r) / r
    fr, fi = np.cos(ang1) / 8.0, -np.sin(ang1) / 8.0
    a_re = np.concatenate([np.kron(fr, eye), np.kron(fi, eye)], axis=0)
    a_im = np.concatenate([np.kron(-fi, eye), np.kron(fr, eye)], axis=0)
    n = r * r
    c = a[:, None, None]
    dd = a[None, :, None]
    bb = a[None, None, :]
    ang2 = 2.0 * np.pi * ((bb * (c + r * dd)) % n) / n
    gr, gi_neg = np.cos(ang2) / 8.0, np.sin(ang2) / 8.0
    b_re = np.zeros((r // nb, r, nb, nb, r))
    b_im = np.zeros((r // nb, r, nb, nb, r))
    for ci in range(nb):
        b_re[:, :, ci, ci, :] = gr[ci::nb]
        b_im[:, :, ci, ci, :] = gi_neg[ci::nb]
    b_re = b_re.reshape(r // nb, r * nb, nb * r)
    b_im = b_im.reshape(r // nb, r * nb, nb * r)
    named = (("cs", cs), ("a_re", a_re), ("a_im", a_im), ("b_re", b_re), ("b_im", b_im))
    return {name: jnp.asarray(t, F32).astype(BF16) for name, t in named}


def _rope_tables(n):
    rows = n // GRID_W
    row = jnp.broadcast_to(jnp.arange(rows, dtype=F32)[:, None], (rows, GRID_W)).reshape(n)
    col = jnp.broadcast_to(jnp.arange(GRID_W, dtype=F32)[None, :], (rows, GRID_W)).reshape(n)
    per_axis = QK_ROPE // 4
    inv_freq = ROPE_BASE ** (-jnp.arange(per_axis, dtype=F32) / per_axis)
    ang = jnp.concatenate([row[:, None] * inv_freq, col[:, None] * inv_freq], axis=-1)
    cos, sin = jnp.cos(ang), jnp.sin(ang)
    tail = LANES - QK_HEAD
    rc = jnp.concatenate([jnp.ones((n, QK_NOPE), F32), cos, cos, jnp.ones((n, tail), F32)], axis=-1)
    rs = jnp.concatenate([jnp.zeros((n, QK_NOPE), F32), -sin, sin, jnp.zeros((n, tail), F32)], axis=-1)
    return rc, rs


def _swap_rope_halves(t):
    r1 = t[..., QK_NOPE:QK_NOPE + ROPE_HALF]
    r2 = t[..., QK_NOPE + ROPE_HALF:]
    return jnp.concatenate([jnp.zeros_like(t[..., :QK_NOPE]), r2, r1], axis=-1)


def _pad_heads(w, width):
    r = w.shape[0]
    return jnp.pad(w, ((0, 0), (0, 0), (0, HEAD_PAD - width))).reshape(r, MLA_HEADS * HEAD_PAD)


def _pad_gain(g):
    return jnp.pad(g, (0, HEAD_PAD - QK_HEAD)).reshape(1, HEAD_PAD)


def _pad_values(w):
    r = w.shape[0]
    pairs = w.reshape(r, MLA_HEADS // 2, 2, V_HEAD)
    zero = jnp.zeros_like(pairs[:, :, 0])
    even = jnp.concatenate([pairs[:, :, 0], zero], axis=-1)
    odd = jnp.concatenate([zero, pairs[:, :, 1]], axis=-1)
    return jnp.stack([even, odd], axis=2).reshape(r, MLA_HEADS * HEAD_PAD)


def _value_ones_row():
    row = np.zeros((MLA_HEADS, HEAD_PAD), np.float32)
    row[0::2, V_HEAD] = 1.0
    row[1::2, 0] = 1.0
    return jnp.asarray(row.reshape(1, MLA_HEADS * HEAD_PAD))


def kernel(x, c, ctx, c_ctx, ada_w, ada_b, norm1_g, norm2_g, ffn_w1, ffn_w3, ffn_w2, a_w_in, a_q_ln_g, a_kv_ln_g, a_w_uq, a_w_uk, a_w_uv, a_q_norm_g, a_k_norm_g, a_w_out, b_w_in, b_conf_dw, b_conf_dw_b, b_conf_ln_g, b_conf_ln_b, b_sc_dw, b_w_out):
    bsz, n, d = x.shape
    depth = ada_w.shape[0]
    assert depth == 2 and d == D_MODEL and n == FFT_R * FFT_R and ctx.shape[1] == CTX_LEN

    rows = 16
    cvec = jnp.concatenate([c, c_ctx[None, :], jnp.zeros((rows - bsz - 1, d), F32)], axis=0)
    ada = _ada_call(cvec, ada_w, ada_b).reshape(depth, rows, 6, d)
    unit = jnp.array([0.0, 1.0, 0.0, 0.0, 1.0, 0.0], F32)[None, None, :, None]
    mods = jnp.pad(ada + unit, ((0, 0), (0, 0), (0, 2), (0, 0)))

    tables = _dft_tables()
    tables["rc"], tables["rs"] = _rope_tables(n)
    tables["ones"] = jnp.ones((LANES, LANES), BF16)

    w_in = a_w_in[0]
    kr_cols = jnp.pad(w_in[:, Q_LORA + KV_LORA:EVEN_KV_END], ((0, 0), (QK_NOPE, LANES - QK_HEAD)))
    w = {
        "wq": jnp.concatenate([w_in[:, :Q_LORA], kr_cols], axis=1).astype(BF16),
        "wkv": w_in[:, Q_LORA:Q_LORA + KV_LORA].astype(BF16),
        "wf": w_in[:, EVEN_KV_END:].astype(BF16),
        "qlg": a_q_ln_g[0].reshape(1, Q_LORA),
        "kvlg": a_kv_ln_g[0].reshape(1, KV_LORA),
        "wuq": _pad_heads(a_w_uq[0], QK_HEAD).astype(BF16),
        "wuqs": _pad_heads(_swap_rope_halves(a_w_uq[0]), QK_HEAD).astype(BF16),
        "wuk": _pad_heads(a_w_uk[0], QK_NOPE).astype(BF16),
        "wuv": _pad_values(a_w_uv[0]).astype(BF16),
        "vone": _value_ones_row(),
        "gq": _pad_gain(a_q_norm_g[0]),
        "gqs": _pad_gain(_swap_rope_halves(a_q_norm_g[0])),
        "gk": _pad_gain(a_k_norm_g[0]),
    }
    g1 = norm1_g[0].reshape(1, d)
    q, k, v, z = _even_in_call(x, mods[0, :bsz], g1, w, tables)
    k_ctx, v_ctx = _ctx_in_call(ctx, mods[0, bsz:bsz + 1], g1, w, tables)
    y = _seq_dft_call(z, tables)
    o = _attn_call(q, k_ctx, k, v_ctx, v)
    x = _even_out_call(x, o, y, mods[0, :bsz], norm2_g[0].reshape(1, d), a_w_out[0].astype(BF16),
                       ffn_w1[0].astype(BF16), ffn_w3[0].astype(BF16), ffn_w2[0].astype(BF16))

    x = _odd_call(x, mods[1, :bsz], norm1_g[1].reshape(1, d), b_w_in[0].astype(BF16),
                  b_conf_dw[0], b_conf_dw_b[0].reshape(1, CONF_W), b_conf_ln_g[0].reshape(1, CONF_W),
                  b_conf_ln_b[0].reshape(1, CONF_W), b_sc_dw[0], b_w_out[0].astype(BF16),
                  norm2_g[1].reshape(1, d), ffn_w1[1].astype(BF16), ffn_w3[1].astype(BF16),
                  ffn_w2[1].astype(BF16))
    return x
```

```python
import functools
import math

import numpy as np
import jax
import jax.numpy as jnp
from jax import lax
from jax.experimental import pallas as pl
from jax.experimental.pallas import tpu as pltpu

F32 = jnp.float32
BF16 = jnp.bfloat16

D_MODEL = 1024
CTX_LEN = 256
GRID_W = 64
EPS = 1e-6
MLA_HEADS = 8
QK_NOPE = 64
QK_ROPE = 32
QK_HEAD = QK_NOPE + QK_ROPE
V_HEAD = 64
Q_LORA = 384
KV_LORA = 256
ROPE_BASE = 10000.0
FNET_GROUPS = 4
FNET_GROUP_W = 128
FNET_W = FNET_GROUPS * FNET_GROUP_W
CONF_GROUPS = 4
CONF_W = 512
CONF_WIDTH = 31
SC_W = 512
SC_WIDTH = 3
D_FF = 2816
EVEN_KV_END = Q_LORA + KV_LORA + QK_ROPE

LANES = 128
HEAD_PAD = LANES
ROPE_HALF = QK_ROPE // 2
FFT_R = 64
HALO = 16
VMEM_LIMIT = 56 * 1024 * 1024

TM_EVEN_IN = 512
TQ_ATTN = 512
KC_ATTN = 512
TM_OUT = 512
TM_ODD = 512
FF_CHUNKS = tuple((c, min(512, D_FF - c)) for c in range(0, D_FF, 512))

Q_SCALE = QK_HEAD ** -0.5 * math.log2(math.e)


def _dot(a, b):
    return jnp.dot(a, b, preferred_element_type=F32)


def _dot_nt(a, b):
    return lax.dot_general(a, b, (((1,), (1,)), ((), ())), preferred_element_type=F32)


def _inv_rms(x, n):
    return lax.rsqrt(jnp.sum(x * x, axis=-1, keepdims=True) * (1.0 / n) + EPS)


def _norm_mod(x, g, shift, scale1p):
    return (x * _inv_rms(x, x.shape[-1]) * g) * scale1p + shift


def _silu(a):
    return a * jax.nn.sigmoid(a)


def _const_spec(shape):
    nd = len(shape)
    return pl.BlockSpec(shape, lambda *_: (0,) * nd, pipeline_mode=pl.Buffered(1))


def _params(sem, flags=None):
    return pltpu.CompilerParams(dimension_semantics=sem, vmem_limit_bytes=VMEM_LIMIT, flags=flags)


def _ada_kernel(c_ref, w_ref, b_ref, o_ref):
    s = _silu(c_ref[...]).astype(BF16)
    o_ref[0] = _dot(s, w_ref[0].astype(BF16)) + b_ref[0]


def _ada_call(cvec, ada_w, ada_b):
    depth, d, n6 = ada_w.shape
    rows = cvec.shape[0]
    tn = 1536
    return pl.pallas_call(
        _ada_kernel,
        grid=(depth, n6 // tn),
        in_specs=[
            pl.BlockSpec((rows, d), lambda l, j: (0, 0)),
            pl.BlockSpec((1, d, tn), lambda l, j: (l, 0, j)),
            pl.BlockSpec((1, 1, tn), lambda l, j: (l, 0, j)),
        ],
        out_specs=pl.BlockSpec((1, rows, tn), lambda l, j: (l, 0, j)),
        out_shape=jax.ShapeDtypeStruct((depth, rows, n6), F32),
        compiler_params=_params(("arbitrary", "arbitrary")),
        name="ada_mod",
    )(cvec, ada_w, ada_b.reshape(depth, 1, n6))


def _head_inv_rms(t, ones):
    ss = _dot((t * t).astype(BF16), ones)
    return lax.rsqrt(ss * (1.0 / QK_HEAD) + EPS)


def _keys_values(hb, krb, rope, wkv_ref, kvlg_ref, wuk_ref, wuv_ref, vone_ref, gk_ref, ones_ref, k_ref, v_ref):
    ckv = _dot(hb, wkv_ref[...])
    ckvn = (ckv * _inv_rms(ckv, KV_LORA) * kvlg_ref[...]).astype(BF16)
    kf = _dot(ckvn, wuk_ref[...])
    v_ref[0] = (_dot(ckvn, wuv_ref[...]) + vone_ref[...]).astype(BF16)
    gk = gk_ref[...]
    krg = krb * gk
    if rope is not None:
        rc, rs = rope
        lane = lax.broadcasted_iota(jnp.int32, krg.shape, 1)
        partner = jnp.where(lane < QK_NOPE + ROPE_HALF, pltpu.roll(krg, LANES - ROPE_HALF, 1),
                            pltpu.roll(krg, ROPE_HALF, 1))
        krg = krg * rc + partner * rs
    ones = ones_ref[...]
    for h in range(MLA_HEADS):
        kn = kf[:, h * HEAD_PAD:(h + 1) * HEAD_PAD]
        r = _head_inv_rms(kn + krb, ones)
        k_ref[0, :, h * HEAD_PAD:(h + 1) * HEAD_PAD] = ((kn * gk + krg) * r).astype(BF16)


def _even_in_kernel(x_ref, mod_ref, g_ref, wq_ref, wkv_ref, wf_ref, qlg_ref, kvlg_ref, wuq_ref, wuqs_ref, wuk_ref,
                    wuv_ref, vone_ref, gq_ref, gqs_ref, gk_ref, rc_ref, rs_ref, cs_ref, ones_ref,
                    q_ref, k_ref, v_ref, z_ref):
    hb = _norm_mod(x_ref[0], g_ref[...], mod_ref[0, 0:1, :], mod_ref[0, 1:2, :]).astype(BF16)
    rc, rs = rc_ref[...], rs_ref[...]
    uq = _dot(hb, wq_ref[...])
    _keys_values(hb, uq[:, Q_LORA:], (rc, rs), wkv_ref, kvlg_ref, wuk_ref, wuv_ref, vone_ref, gk_ref, ones_ref,
                 k_ref, v_ref)

    cq = uq[:, :Q_LORA]
    cqn = (cq * _inv_rms(cq, Q_LORA) * qlg_ref[...]).astype(BF16)
    qf = _dot(cqn, wuq_ref[...])
    qs = _dot(cqn, wuqs_ref[...])
    gq = gq_ref[...] * Q_SCALE
    gqs = gqs_ref[...] * Q_SCALE
    ones = ones_ref[...]
    for h in range(MLA_HEADS):
        sl = slice(h * HEAD_PAD, (h + 1) * HEAD_PAD)
        qh = qf[:, sl]
        r = _head_inv_rms(qh, ones)
        q_ref[0, :, sl] = ((qh * gq * rc + qs[:, sl] * gqs * rs) * r).astype(BF16)

    uf = _dot(hb, wf_ref[...]).astype(BF16)
    for g in range(FNET_GROUPS):
        zz = _dot(uf[:, g * LANES:(g + 1) * LANES], cs_ref[...])
        z_ref[0, :, g * LANES:(g + 1) * LANES] = zz[:, :LANES]
        z_ref[0, :, FNET_W + g * LANES:FNET_W + (g + 1) * LANES] = zz[:, LANES:]


def _ctx_in_kernel(x_ref, mod_ref, g_ref, wq_ref, wkv_ref, kvlg_ref, wuk_ref, wuv_ref, vone_ref, gk_ref, ones_ref,
                   k_ref, v_ref):
    hb = _norm_mod(x_ref[0], g_ref[...], mod_ref[0, 0:1, :], mod_ref[0, 1:2, :]).astype(BF16)
    krb = _dot(hb, wq_ref[:, Q_LORA:])
    _keys_values(hb, krb, None, wkv_ref, kvlg_ref, wuk_ref, wuv_ref, vone_ref, gk_ref, ones_ref, k_ref, v_ref)


def _even_in_call(x, mod, g, w, tables):
    bsz, n, d = x.shape
    tm = TM_EVEN_IN
    hw = MLA_HEADS * HEAD_PAD
    tok = lambda width: pl.BlockSpec((1, tm, width), lambda b, i: (b, i, 0))
    tab = pl.BlockSpec((tm, LANES), lambda b, i: (i, 0))
    return pl.pallas_call(
        _even_in_kernel,
        grid=(bsz, n // tm),
        in_specs=[
            tok(d),
            pl.BlockSpec((1, 8, d), lambda b, i: (b, 0, 0)),
            _const_spec((1, d)),
            _const_spec(w["wq"].shape), _const_spec(w["wkv"].shape), _const_spec(w["wf"].shape),
            _const_spec((1, Q_LORA)), _const_spec((1, KV_LORA)),
            _const_spec(w["wuq"].shape), _const_spec(w["wuqs"].shape), _const_spec(w["wuk"].shape),
            _const_spec(w["wuv"].shape), _const_spec((1, hw)),
            _const_spec((1, LANES)), _const_spec((1, LANES)), _const_spec((1, LANES)),
            tab, tab,
            _const_spec(tables["cs"].shape), _const_spec(tables["ones"].shape),
        ],
        out_specs=[tok(hw), tok(hw), tok(hw), tok(2 * FNET_W)],
        out_shape=[
            jax.ShapeDtypeStruct((bsz, n, hw), BF16),
            jax.ShapeDtypeStruct((bsz, n, hw), BF16),
            jax.ShapeDtypeStruct((bsz, n, hw), BF16),
            jax.ShapeDtypeStruct((bsz, n, 2 * FNET_W), F32),
        ],
        compiler_params=_params(("parallel", "parallel")),
        name="even_in",
    )(x, mod, g, w["wq"], w["wkv"], w["wf"], w["qlg"], w["kvlg"], w["wuq"], w["wuqs"], w["wuk"], w["wuv"],
      w["vone"], w["gq"], w["gqs"], w["gk"], tables["rc"], tables["rs"], tables["cs"], tables["ones"])


def _ctx_in_call(ctx, mod, g, w, tables):
    bsz, n, d = ctx.shape
    hw = MLA_HEADS * HEAD_PAD
    tok = lambda width: pl.BlockSpec((1, n, width), lambda b: (b, 0, 0))
    return pl.pallas_call(
        _ctx_in_kernel,
        grid=(bsz,),
        in_specs=[
            tok(d),
            pl.BlockSpec((1, 8, d), lambda b: (0, 0, 0)),
            _const_spec((1, d)),
            _const_spec(w["wq"].shape), _const_spec(w["wkv"].shape),
            _const_spec((1, KV_LORA)),
            _const_spec(w["wuk"].shape), _const_spec(w["wuv"].shape), _const_spec((1, hw)),
            _const_spec((1, LANES)), _const_spec(tables["ones"].shape),
        ],
        out_specs=[tok(hw), tok(hw)],
        out_shape=[
            jax.ShapeDtypeStruct((bsz, n, hw), BF16),
            jax.ShapeDtypeStruct((bsz, n, hw), BF16),
        ],
        compiler_params=_params(("parallel",)),
        name="ctx_in",
    )(ctx, mod, g, w["wq"], w["wkv"], w["kvlg"], w["wuk"], w["wuv"], w["vone"], w["gk"], tables["ones"])


DFT_NB = 8


def _seq_dft_kernel(z_ref, are_ref, aim_ref, bre_ref, bim_ref, y_ref, t_scr):
    r = FFT_R
    half = r // DFT_NB
    s = pl.program_id(1)

    @pl.when(s < half)
    def _():
        z = z_ref[0].reshape(r * DFT_NB, 2 * FNET_W).astype(BF16)
        t = _dot(are_ref[...], z[:, :FNET_W]) + _dot(aim_ref[...], z[:, FNET_W:])
        t_scr[:, pl.ds(pl.multiple_of(s * DFT_NB, DFT_NB), DFT_NB), :] = t.reshape(2 * r, DFT_NB, FNET_W)

    @pl.when(s >= half)
    def _():
        j = s - half
        c0 = pl.multiple_of(j * DFT_NB, DFT_NB)
        xr = t_scr[pl.ds(c0, DFT_NB)].reshape(DFT_NB * r, FNET_W).astype(BF16)
        xi = t_scr[pl.ds(r + c0, DFT_NB)].reshape(DFT_NB * r, FNET_W).astype(BF16)
        y = _dot(bre_ref[j], xr) + _dot(bim_ref[j], xi)
        y_ref[0] = y.reshape(y_ref.shape[1:])


def _seq_dft_call(z, tables):
    bsz, n, w2 = z.shape
    r = FFT_R
    nb = DFT_NB
    half = r // nb
    y = pl.pallas_call(
        _seq_dft_kernel,
        grid=(bsz, 2 * half),
        in_specs=[
            pl.BlockSpec((1, r, nb, w2), lambda b, s: (b, 0, jnp.minimum(s, half - 1), 0)),
            _const_spec(tables["a_re"].shape), _const_spec(tables["a_im"].shape),
            _const_spec(tables["b_re"].shape), _const_spec(tables["b_im"].shape),
        ],
        out_specs=pl.BlockSpec((1, r, nb, FNET_W), lambda b, s: (b, 0, jnp.maximum(s - half, 0), 0)),
        out_shape=jax.ShapeDtypeStruct((bsz, r, r, FNET_W), F32),
        scratch_shapes=[pltpu.VMEM((2 * r, r, FNET_W), F32)],
        compiler_params=_params(("parallel", "arbitrary")),
        name="seq_dft",
    )(z.reshape(bsz, r, r, w2), tables["a_re"], tables["a_im"], tables["b_re"], tables["b_im"])
    return y.reshape(bsz, n, FNET_W)


def _attn_kernel(q_ref, kc_ref, k_ref, vc_ref, v_ref, qn_ref, kcn_ref, kn_ref, o_ref,
                 s0_ref, s1_ref, m0_ref, m1_ref):
    n_ctx = kc_ref.shape[1]
    n_chunks = k_ref.shape[1] // KC_ATTN
    tq = TQ_ATTN
    n_tiles = q_ref.shape[1] // tq
    s_refs = (s0_ref, s1_ref)
    m_refs = (m0_ref, m1_ref)
    chunks = [(True, 0, n_ctx, 0)] + [(False, j * KC_ATTN, KC_ATTN, n_ctx + j * KC_ATTN) for j in range(n_chunks)]

    def lane_groups(x):
        return [x[:, t * LANES:(t + 1) * LANES] for t in range(x.shape[1] // LANES)]

    def segment(q, key_refs, h_scores, h_values):
        lo = h_scores * HEAD_PAD
        s_out = s_refs[h_scores]
        m_run = acc = None
        if h_values is not None:
            s_in = s_refs[h_values]
            m_in = m_refs[h_values][...]
            lv = h_values * HEAD_PAD
        for is_ctx, r0, width, c0 in chunks:
            if h_values is not None:
                vr = vc_ref if is_ctx else v_ref
                p = [jnp.exp2(g - m_in) for g in lane_groups(s_in[:, c0:c0 + width])]
                pv = _dot(jnp.concatenate(p, axis=1).astype(BF16), vr[0, r0:r0 + width, lv:lv + HEAD_PAD])
                acc = pv if acc is None else acc + pv
            kr = key_refs[0] if is_ctx else key_refs[1]
            s = _dot_nt(q, kr[0, r0:r0 + width, lo:lo + HEAD_PAD])
            s_out[:, c0:c0 + width] = s
            m_run = functools.reduce(jnp.maximum, lane_groups(s) if m_run is None else [m_run] + lane_groups(s))
        m_refs[h_scores][...] = jnp.broadcast_to(jnp.max(m_run, axis=-1, keepdims=True), (tq, LANES))
        return acc

    def q_tile(i, hh):
        return q_ref[0, i * tq:(i + 1) * tq, hh * HEAD_PAD:(hh + 1) * HEAD_PAD]

    here = (kc_ref, k_ref)

    @pl.when((pl.program_id(0) == 0) & (pl.program_id(1) == 0))
    def _():
        segment(q_tile(0, 0), here, 0, None)

    for i in range(n_tiles):
        acc0 = segment(q_tile(i, 1), here, 1, 0)
        if i + 1 < n_tiles:
            acc1 = segment(q_tile(i + 1, 0), here, 0, 1)
        else:
            acc1 = segment(qn_ref[0, :, 0:HEAD_PAD], (kcn_ref, kn_ref), 0, 1)
        out0 = acc0 / acc0[:, V_HEAD:V_HEAD + 1]
        out1 = acc1 / acc1[:, 0:1]
        lane = lax.broadcasted_iota(jnp.int32, out0.shape, 1)
        o_ref[0, i * tq:(i + 1) * tq, :] = jnp.where(lane < V_HEAD, out0, out1).astype(BF16)


def _attn_call(q, k_ctx, k, v_ctx, v):
    bsz, n, hw = q.shape
    n_ctx = k_ctx.shape[1]
    tq = TQ_ATTN
    pair = 2 * HEAD_PAD
    n_pairs = MLA_HEADS // 2

    def cur(b, h):
        return b, 0, h

    def nxt(b, h):
        f = jnp.minimum(b * n_pairs + h + 1, bsz * n_pairs - 1)
        return f // n_pairs, 0, f % n_pairs

    return pl.pallas_call(
        _attn_kernel,
        grid=(bsz, n_pairs),
        in_specs=[
            pl.BlockSpec((1, n, pair), cur),
            pl.BlockSpec((1, n_ctx, pair), cur),
            pl.BlockSpec((1, n, pair), cur),
            pl.BlockSpec((1, n_ctx, pair), cur),
            pl.BlockSpec((1, n, pair), cur),
            pl.BlockSpec((1, tq, pair), nxt),
            pl.BlockSpec((1, n_ctx, pair), nxt),
            pl.BlockSpec((1, n, pair), nxt),
        ],
        out_specs=pl.BlockSpec((1, n, 2 * V_HEAD), cur),
        out_shape=jax.ShapeDtypeStruct((bsz, n, MLA_HEADS * V_HEAD), BF16),
        scratch_shapes=[pltpu.VMEM((tq, n_ctx + n), F32), pltpu.VMEM((tq, n_ctx + n), F32),
                        pltpu.VMEM((tq, LANES), F32), pltpu.VMEM((tq, LANES), F32)],
        compiler_params=_params(("arbitrary", "arbitrary")),
        name="attention",
    )(q, k_ctx, k, v_ctx, v, q, k_ctx, k)


def _ffn_tail(x1, mod_ref, g2n_ref, w1_ref, w3_ref, w2_ref, out_ref):
    h2 = _norm_mod(x1, g2n_ref[...], mod_ref[0, 3:4, :], mod_ref[0, 4:5, :]).astype(BF16)
    acc = None
    for c0, cw in FF_CHUNKS:
        a = _dot(h2, w1_ref[:, c0:c0 + cw])
        b = _dot(h2, w3_ref[:, c0:c0 + cw])
        part = _dot((_silu(a) * b).astype(BF16), w2_ref[c0:c0 + cw, :])
        acc = part if acc is None else acc + part
    out_ref[0] = x1 + mod_ref[0, 5:6, :] * acc


def _even_out_kernel(x_ref, o_ref, y_ref, mod_ref, g2n_ref, wo_ref, w1_ref, w3_ref, w2_ref, out_ref):
    hv = MLA_HEADS * V_HEAD
    mix = _dot(o_ref[0], wo_ref[0:hv, :]) + _dot(y_ref[0].astype(BF16), wo_ref[hv:, :])
    x1 = x_ref[0] + mod_ref[0, 2:3, :] * mix
    _ffn_tail(x1, mod_ref, g2n_ref, w1_ref, w3_ref, w2_ref, out_ref)


def _even_out_call(x, o, y, mod, g2n, wo, w1, w3, w2):
    bsz, n, d = x.shape
    tm = TM_OUT
    tok = lambda width: pl.BlockSpec((1, tm, width), lambda b, i: (b, i, 0))
    return pl.pallas_call(
        _even_out_kernel,
        grid=(bsz, n // tm),
        in_specs=[
            tok(d), tok(o.shape[-1]), tok(y.shape[-1]),
            pl.BlockSpec((1, 8, d), lambda b, i: (b, 0, 0)),
            _const_spec((1, d)),
            _const_spec(wo.shape), _const_spec(w1.shape), _const_spec(w3.shape), _const_spec(w2.shape),
        ],
        out_specs=tok(d),
        out_shape=jax.ShapeDtypeStruct((bsz, n, d), F32),
        compiler_params=_params(("parallel", "parallel")),
        name="even_out_ffn",
    )(x, o, y, mod, g2n, wo, w1, w3, w2)


CONV_RB = 64


SUBLANES = 8


def _conf_conv_block(cbuf, dw_ref, bias, r0, ls):
    rows = CONV_RB + SUBLANES
    acc = jnp.broadcast_to(bias, (CONV_RB, LANES))
    for r in range(SUBLANES):
        part = None
        for o in range(r, CONF_WIDTH + 1, SUBLANES):
            if o == 0:
                continue
            term = dw_ref[o - 1:o, ls] * cbuf[r0 + o - r:r0 + o - r + rows, ls]
            part = term if part is None else part + term
        acc = acc + part[r:r + CONV_RB]
    return acc


def _zero_row(val_row, zero_ref):
    bits = lax.bitcast_convert_type(val_row, jnp.int32) & zero_ref[...]
    return lax.bitcast_convert_type(bits, F32)


def _odd_kernel(xp_ref, x_ref, xn_ref, xlag_ref, mod_ref, modlag_ref, zero_ref, g1n_ref, win_ref, dw_ref, dwb_ref,
                lng_ref, lnb_ref, scw_ref, wout_ref, g2n_ref, w1_ref, w3_ref, w2_ref, out_ref,
                hb_ref, cbuf, sbuf, bbuf, mbuf, acc_ref, *, tiles_per_seq):
    tm = x_ref.shape[1]
    t = pl.program_id(0)

    @pl.when(t == 0)
    def _():
        mbuf[...] = jnp.zeros(mbuf.shape, mbuf.dtype)

    mix = _dot(mbuf[...], wout_ref[...])
    x1 = xlag_ref[0] + modlag_ref[0, 2:3, :] * mix
    h2 = _norm_mod(x1, g2n_ref[...], modlag_ref[0, 3:4, :], modlag_ref[0, 4:5, :]).astype(BF16)

    i = lax.rem(jnp.minimum(t, pl.num_programs(0) - 2), tiles_per_seq)
    last = tiles_per_seq - 1
    g = g1n_ref[...]
    sh = mod_ref[0, 0:1, :]
    sc = mod_ref[0, 1:2, :]
    hb_ref[0:HALO, :] = _norm_mod(xp_ref[0], g, sh, sc).astype(BF16)
    hb_ref[HALO:HALO + tm, :] = _norm_mod(x_ref[0], g, sh, sc).astype(BF16)
    hb_ref[HALO + tm:, :] = _norm_mod(xn_ref[0], g, sh, sc).astype(BF16)
    u = _dot(hb_ref[...], win_ref[...])

    conf = u[:, 0:CONF_W] * jax.nn.sigmoid(u[:, CONF_W:2 * CONF_W])
    cx = u[:, 2 * CONF_W + SC_W:2 * CONF_W + 2 * SC_W] * u[:, 2 * CONF_W + 2 * SC_W:]
    keep_lo = i > 0
    keep_hi = i < last
    cbuf[0:HALO, :] = jnp.where(keep_lo, conf[0:HALO], 0.0)
    cbuf[HALO:HALO + tm, :] = conf[HALO:HALO + tm]
    cbuf[HALO + tm:, :] = jnp.where(keep_hi, conf[HALO + tm:], 0.0)
    sbuf[0:HALO, :] = jnp.where(keep_lo, cx[0:HALO], 0.0)
    sbuf[HALO:HALO + tm, :] = cx[HALO:HALO + tm]
    sbuf[HALO + tm:, :] = jnp.where(keep_hi, cx[HALO + tm:], 0.0)
    bbuf[...] = u[HALO:HALO + tm, 2 * CONF_W:2 * CONF_W + SC_W]

    assert HALO == (CONF_WIDTH - 1) // 2 + 1
    pad_s = (SC_WIDTH - 1) // 2

    def conv_block(r0, lg, order_after):
        ls = slice(lg * LANES, (lg + 1) * LANES)
        bias = dwb_ref[:, ls]
        if order_after is not None:
            bias = bias + order_after
        acc = _conf_conv_block(cbuf, dw_ref, bias, r0, ls)
        mu = jnp.mean(acc, axis=-1, keepdims=True)
        cen = acc - mu
        var = jnp.mean(cen * cen, axis=-1, keepdims=True)
        yn = cen * lax.rsqrt(var + EPS) * lng_ref[:, ls] + lnb_ref[:, ls]
        mbuf[r0:r0 + CONV_RB, ls] = _silu(yn).astype(BF16)

        acc = None
        for j in range(SC_WIDTH):
            o = HALO + r0 + j - pad_s
            term = scw_ref[j:j + 1, ls] * sbuf[o:o + CONV_RB, ls]
            acc = term if acc is None else acc + term
        mbuf[r0:r0 + CONV_RB, CONF_W + lg * LANES:CONF_W + (lg + 1) * LANES] = (
            bbuf[r0:r0 + CONV_RB, ls] * acc).astype(BF16)
        return yn[0:1, :]

    blocks = [(r0, lg) for r0 in range(0, tm, CONV_RB) for lg in range(CONF_W // LANES)]
    n_groups = len(FF_CHUNKS) - 1
    after_conv = None
    for k, (c0, cw) in enumerate(FF_CHUNKS):
        a = _dot(h2, w1_ref[:, c0:c0 + cw])
        b = _dot(h2, w3_ref[:, c0:c0 + cw])
        part = _dot((_silu(a) * b).astype(BF16), w2_ref[c0:c0 + cw, :])
        if k == 0:
            acc_ref[...] = part
        else:
            acc_ref[...] += part
        if after_conv is not None:
            acc_ref[0:1, 0:LANES] += after_conv
            after_conv = None
        if k < n_groups:
            after_chunk = _zero_row(part[0:1, 0:LANES], zero_ref)
            tail = None
            for r0, lg in blocks[k * len(blocks) // n_groups:(k + 1) * len(blocks) // n_groups]:
                tail = conv_block(r0, lg, after_chunk)
            after_conv = _zero_row(tail, zero_ref)
    out_ref[0] = x1 + modlag_ref[0, 5:6, :] * acc_ref[...]


def _odd_call(x, mod, g1n, win, dw, dwb, lng, lnb, scw, wout, g2n, w1, w3, w2):
    bsz, n, d = x.shape
    tm = TM_ODD
    hb = tm // HALO
    nh = n // HALO
    nt = n // tm
    total = bsz * nt

    def cur(t):
        u = jnp.minimum(t, total - 1)
        return u // nt, u % nt

    def lag(t):
        u = jnp.maximum(t - 1, 0)
        return u // nt, u % nt

    def halo_lo(t):
        b, i = cur(t)
        return b, jnp.maximum(i * hb - 1, 0), 0

    def halo_hi(t):
        b, i = cur(t)
        return b, jnp.minimum((i + 1) * hb, nh - 1), 0

    return pl.pallas_call(
        functools.partial(_odd_kernel, tiles_per_seq=nt),
        grid=(total + 1,),
        in_specs=[
            pl.BlockSpec((1, HALO, d), halo_lo),
            pl.BlockSpec((1, tm, d), lambda t: (*cur(t), 0)),
            pl.BlockSpec((1, HALO, d), halo_hi),
            pl.BlockSpec((1, tm, d), lambda t: (*lag(t), 0)),
            pl.BlockSpec((1, 8, d), lambda t: (cur(t)[0], 0, 0)),
            pl.BlockSpec((1, 8, d), lambda t: (lag(t)[0], 0, 0)),
            _const_spec((1, LANES)),
            _const_spec((1, d)),
            _const_spec(win.shape),
            _const_spec(dw.shape), _const_spec(dwb.shape), _const_spec(lng.shape), _const_spec(lnb.shape),
            _const_spec(scw.shape),
            _const_spec(wout.shape),
            _const_spec((1, d)),
            _const_spec(w1.shape), _const_spec(w3.shape), _const_spec(w2.shape),
        ],
        out_specs=pl.BlockSpec((1, tm, d), lambda t: (*lag(t), 0)),
        out_shape=jax.ShapeDtypeStruct((bsz, n, d), F32),
        scratch_shapes=[
            pltpu.VMEM((tm + 2 * HALO, d), BF16),
            pltpu.VMEM((tm + 2 * HALO, CONF_W), F32),
            pltpu.VMEM((tm + 2 * HALO, SC_W), F32),
            pltpu.VMEM((tm, SC_W), F32),
            pltpu.VMEM((tm, CONF_W + SC_W), BF16),
            pltpu.VMEM((tm, d), F32),
        ],
        compiler_params=_params(("arbitrary",)),
        name="odd_layer",
    )(x, x, x, x, mod, mod, jnp.zeros((1, LANES), jnp.int32), g1n, win, dw, dwb, lng, lnb, scw, wout, g2n,
      w1, w3, w2)


def _dft_tables():
    r = FFT_R
    gw = FNET_GROUP_W
    idx = np.arange(gw)
    ang = 2.0 * np.pi * ((idx[:, None] * idx[None, :]) % gw) / gw
    cs = np.concatenate([np.cos(ang), -np.sin(ang)], axis=1) / np.sqrt(gw)
    a = np.arange(r)
    nb = DFT_NB
    eye = np.eye(nb)
    ang1 = 2.0 * np.pi * ((a[:, None] * a[None, :]) % r) / r
    fr, fi = np.cos(ang1) / 8.0, -np.sin(ang1) / 8.0
    a_re = np.concatenate([np.kron(fr, eye), np.kron(fi, eye)], axis=0)
    a_im = np.concatenate([np.kron(-fi, eye), np.kron(fr, eye)], axis=0)
    n = r * r
    c = a[:, None, None]
    dd = a[None, :, None]
    bb = a[None, None, :]
    ang2 = 2.0 * np.pi * ((bb * (c + r * dd)) % n) / n
    gr, gi_neg = np.cos(ang2) / 8.0, np.sin(ang2) / 8.0
    b_re = np.zeros((r // nb, r, nb, nb, r))
    b_im = np.zeros((r // nb, r, nb, nb, r))
    for ci in range(nb):
        b_re[:, :, ci, ci, :] = gr[ci::nb]
        b_im[:, :, ci, ci, :] = gi_neg[ci::nb]
    b_re = b_re.reshape(r // nb, r * nb, nb * r)
    b_im = b_im.reshape(r // nb, r * nb, nb * r)
    named = (("cs", cs), ("a_re", a_re), ("a_im", a_im), ("b_re", b_re), ("b_im", b_im))
    return {name: jnp.asarray(t, F32).astype(BF16) for name, t in named}


def _rope_tables(n):
    rows = n // GRID_W
    row = jnp.broadcast_to(jnp.arange(rows, dtype=F32)[:, None], (rows, GRID_W)).reshape(n)
    col = jnp.broadcast_to(jnp.arange(GRID_W, dtype=F32)[None, :], (rows, GRID_W)).reshape(n)
    per_axis = QK_ROPE // 4
    inv_freq = ROPE_BASE ** (-jnp.arange(per_axis, dtype=F32) / per_axis)
    ang = jnp.concatenate([row[:, None] * inv_freq, col[:, None] * inv_freq], axis=-1)
    cos, sin = jnp.cos(ang), jnp.sin(ang)
    tail = LANES - QK_HEAD
    rc = jnp.concatenate([jnp.ones((n, QK_NOPE), F32), cos, cos, jnp.ones((n, tail), F32)], axis=-1)
    rs = jnp.concatenate([jnp.zeros((n, QK_NOPE), F32), -sin, sin, jnp.zeros((n, tail), F32)], axis=-1)
    return rc, rs


def _swap_rope_halves(t):
    r1 = t[..., QK_NOPE:QK_NOPE + ROPE_HALF]
    r2 = t[..., QK_NOPE + ROPE_HALF:]
    return jnp.concatenate([jnp.zeros_like(t[..., :QK_NOPE]), r2, r1], axis=-1)


def _pad_heads(w, width):
    r = w.shape[0]
    return jnp.pad(w, ((0, 0), (0, 0), (0, HEAD_PAD - width))).reshape(r, MLA_HEADS * HEAD_PAD)


def _pad_gain(g):
    return jnp.pad(g, (0, HEAD_PAD - QK_HEAD)).reshape(1, HEAD_PAD)


def _pad_values(w):
    r = w.shape[0]
    pairs = w.reshape(r, MLA_HEADS // 2, 2, V_HEAD)
    zero = jnp.zeros_like(pairs[:, :, 0])
    even = jnp.concatenate([pairs[:, :, 0], zero], axis=-1)
    odd = jnp.concatenate([zero, pairs[:, :, 1]], axis=-1)
    return jnp.stack([even, odd], axis=2).reshape(r, MLA_HEADS * HEAD_PAD)


def _value_ones_row():
    row = np.zeros((MLA_HEADS, HEAD_PAD), np.float32)
    row[0::2, V_HEAD] = 1.0
    row[1::2, 0] = 1.0
    return jnp.asarray(row.reshape(1, MLA_HEADS * HEAD_PAD))


def kernel(x, c, ctx, c_ctx, ada_w, ada_b, norm1_g, norm2_g, ffn_w1, ffn_w3, ffn_w2, a_w_in, a_q_ln_g, a_kv_ln_g, a_w_uq, a_w_uk, a_w_uv, a_q_norm_g, a_k_norm_g, a_w_out, b_w_in, b_conf_dw, b_conf_dw_b, b_conf_ln_g, b_conf_ln_b, b_sc_dw, b_w_out):
    bsz, n, d = x.shape
    depth = ada_w.shape[0]
    assert depth == 2 and d == D_MODEL and n == FFT_R * FFT_R and ctx.shape[1] == CTX_LEN

    rows = 16
    cvec = jnp.concatenate([c, c_ctx[None, :], jnp.zeros((rows - bsz - 1, d), F32)], axis=0)
    ada = _ada_call(cvec, ada_w, ada_b).reshape(depth, rows, 6, d)
    unit = jnp.array([0.0, 1.0, 0.0, 0.0, 1.0, 0.0], F32)[None, None, :, None]
    mods = jnp.pad(ada + unit, ((0, 0), (0, 0), (0, 2), (0, 0)))

    tables = _dft_tables()
    tables["rc"], tables["rs"] = _rope_tables(n)
    tables["ones"] = jnp.ones((LANES, LANES), BF16)

    w_in = a_w_in[0]
    kr_cols = jnp.pad(w_in[:, Q_LORA + KV_LORA:EVEN_KV_END], ((0, 0), (QK_NOPE, LANES - QK_HEAD)))
    w = {
        "wq": jnp.concatenate([w_in[:, :Q_LORA], kr_cols], axis=1).astype(BF16),
        "wkv": w_in[:, Q_LORA:Q_LORA + KV_LORA].astype(BF16),
        "wf": w_in[:, EVEN_KV_END:].astype(BF16),
        "qlg": a_q_ln_g[0].reshape(1, Q_LORA),
        "kvlg": a_kv_ln_g[0].reshape(1, KV_LORA),
        "wuq": _pad_heads(a_w_uq[0], QK_HEAD).astype(BF16),
        "wuqs": _pad_heads(_swap_rope_halves(a_w_uq[0]), QK_HEAD).astype(BF16),
        "wuk": _pad_heads(a_w_uk[0], QK_NOPE).astype(BF16),
        "wuv": _pad_values(a_w_uv[0]).astype(BF16),
        "vone": _value_ones_row(),
        "gq": _pad_gain(a_q_norm_g[0]),
        "gqs": _pad_gain(_swap_rope_halves(a_q_norm_g[0])),
        "gk": _pad_gain(a_k_norm_g[0]),
    }
    g1 = norm1_g[0].reshape(1, d)
    q, k, v, z = _even_in_call(x, mods[0, :bsz], g1, w, tables)
    k_ctx, v_ctx = _ctx_in_call(ctx, mods[0, bsz:bsz + 1], g1, w, tables)
    y = _seq_dft_call(z, tables)
    o = _attn_call(q, k_ctx, k, v_ctx, v)
    x = _even_out_call(x, o, y, mods[0, :bsz], norm2_g[0].reshape(1, d), a_w_out[0].astype(BF16),
                       ffn_w1[0].astype(BF16), ffn_w3[0].astype(BF16), ffn_w2[0].astype(BF16))

    x = _odd_call(x, mods[1, :bsz], norm1_g[1].reshape(1, d), b_w_in[0].astype(BF16),
                  b_conf_dw[0], b_conf_dw_b[0].reshape(1, CONF_W), b_conf_ln_g[0].reshape(1, CONF_W),
                  b_conf_ln_b[0].reshape(1, CONF_W), b_sc_dw[0], b_w_out[0].astype(BF16),
                  norm2_g[1].reshape(1, d), ffn_w1[1].astype(BF16), ffn_w3[1].astype(BF16),
                  ffn_w2[1].astype(BF16))
    return x
```

```python
import functools
import math

import numpy as np
import jax
import jax.numpy as jnp
from jax import lax
from jax.experimental import pallas as pl
from jax.experimental.pallas import tpu as pltpu

F32 = jnp.float32
BF16 = jnp.bfloat16

D_MODEL = 1024
CTX_LEN = 256
GRID_W = 64
EPS = 1e-6
MLA_HEADS = 8
QK_NOPE = 64
QK_ROPE = 32
QK_HEAD = QK_NOPE + QK_ROPE
V_HEAD = 64
Q_LORA = 384
KV_LORA = 256
ROPE_BASE = 10000.0
FNET_GROUPS = 4
FNET_GROUP_W = 128
FNET_W = FNET_GROUPS * FNET_GROUP_W
CONF_GROUPS = 4
CONF_W = 512
CONF_WIDTH = 31
SC_W = 512
SC_WIDTH = 3
D_FF = 2816
EVEN_KV_END = Q_LORA + KV_LORA + QK_ROPE

LANES = 128
HEAD_PAD = LANES
ROPE_HALF = QK_ROPE // 2
FFT_R = 64
HALO = 16
VMEM_LIMIT = 56 * 1024 * 1024

TM_EVEN_IN = 512
TQ_ATTN = 512
KC_ATTN = 512
TM_OUT = 512
TM_ODD = 512
FF_CHUNKS = tuple((c, min(512, D_FF - c)) for c in range(0, D_FF, 512))

Q_SCALE = QK_HEAD ** -0.5 * math.log2(math.e)


def _dot(a, b):
    return jnp.dot(a, b, preferred_element_type=F32)


def _dot_nt(a, b):
    return lax.dot_general(a, b, (((1,), (1,)), ((), ())), preferred_element_type=F32)


def _inv_rms(x, n):
    return lax.rsqrt(jnp.sum(x * x, axis=-1, keepdims=True) * (1.0 / n) + EPS)


def _norm_mod(x, g, shift, scale1p):
    return (x * _inv_rms(x, x.shape[-1]) * g) * scale1p + shift


def _silu(a):
    return a * jax.nn.sigmoid(a)


def _const_spec(shape):
    nd = len(shape)
    return pl.BlockSpec(shape, lambda *_: (0,) * nd, pipeline_mode=pl.Buffered(1))


def _params(sem, flags=None):
    return pltpu.CompilerParams(dimension_semantics=sem, vmem_limit_bytes=VMEM_LIMIT, flags=flags)


def _ada_kernel(c_ref, w_ref, b_ref, o_ref):
    s = _silu(c_ref[...]).astype(BF16)
    o_ref[0] = _dot(s, w_ref[0].astype(BF16)) + b_ref[0]


def _ada_call(cvec, ada_w, ada_b):
    depth, d, n6 = ada_w.shape
    rows = cvec.shape[0]
    tn = 1536
    return pl.pallas_call(
        _ada_kernel,
        grid=(depth, n6 // tn),
        in_specs=[
            pl.BlockSpec((rows, d), lambda l, j: (0, 0)),
            pl.BlockSpec((1, d, tn), lambda l, j: (l, 0, j)),
            pl.BlockSpec((1, 1, tn), lambda l, j: (l, 0, j)),
        ],
        out_specs=pl.BlockSpec((1, rows, tn), lambda l, j: (l, 0, j)),
        out_shape=jax.ShapeDtypeStruct((depth, rows, n6), F32),
        compiler_params=_params(("arbitrary", "arbitrary")),
        name="ada_mod",
    )(cvec, ada_w, ada_b.reshape(depth, 1, n6))


def _head_inv_rms(t, ones):
    ss = _dot((t * t).astype(BF16), ones)
    return lax.rsqrt(ss * (1.0 / QK_HEAD) + EPS)


def _keys_values(hb, krb, rope, wkv_ref, kvlg_ref, wuk_ref, wuv_ref, vone_ref, gk_ref, ones_ref, k_ref, v_ref):
    ckv = _dot(hb, wkv_ref[...])
    ckvn = (ckv * _inv_rms(ckv, KV_LORA) * kvlg_ref[...]).astype(BF16)
    kf = _dot(ckvn, wuk_ref[...])
    v_ref[0] = (_dot(ckvn, wuv_ref[...]) + vone_ref[...]).astype(BF16)
    gk = gk_ref[...]
    krg = krb * gk
    if rope is not None:
        rc, rs = rope
        lane = lax.broadcasted_iota(jnp.int32, krg.shape, 1)
        partner = jnp.where(lane < QK_NOPE + ROPE_HALF, pltpu.roll(krg, LANES - ROPE_HALF, 1),
                            pltpu.roll(krg, ROPE_HALF, 1))
        krg = krg * rc + partner * rs
    ones = ones_ref[...]
    for h in range(MLA_HEADS):
        kn = kf[:, h * HEAD_PAD:(h + 1) * HEAD_PAD]
        r = _head_inv_rms(kn + krb, ones)
        k_ref[0, :, h * HEAD_PAD:(h + 1) * HEAD_PAD] = ((kn * gk + krg) * r).astype(BF16)


def _even_in_kernel(x_ref, mod_ref, g_ref, wq_ref, wkv_ref, wf_ref, qlg_ref, kvlg_ref, wuq_ref, wuqs_ref, wuk_ref,
                    wuv_ref, vone_ref, gq_ref, gqs_ref, gk_ref, rc_ref, rs_ref, cs_ref, ones_ref,
                    q_ref, k_ref, v_ref, z_ref):
    hb = _norm_mod(x_ref[0], g_ref[...], mod_ref[0, 0:1, :], mod_ref[0, 1:2, :]).astype(BF16)
    rc, rs = rc_ref[...], rs_ref[...]
    uq = _dot(hb, wq_ref[...])
    _keys_values(hb, uq[:, Q_LORA:], (rc, rs), wkv_ref, kvlg_ref, wuk_ref, wuv_ref, vone_ref, gk_ref, ones_ref,
                 k_ref, v_ref)

    cq = uq[:, :Q_LORA]
    cqn = (cq * _inv_rms(cq, Q_LORA) * qlg_ref[...]).astype(BF16)
    qf = _dot(cqn, wuq_ref[...])
    qs = _dot(cqn, wuqs_ref[...])
    gq = gq_ref[...] * Q_SCALE
    gqs = gqs_ref[...] * Q_SCALE
    ones = ones_ref[...]
    for h in range(MLA_HEADS):
        sl = slice(h * HEAD_PAD, (h + 1) * HEAD_PAD)
        qh = qf[:, sl]
        r = _head_inv_rms(qh, ones)
        q_ref[0, :, sl] = ((qh * gq * rc + qs[:, sl] * gqs * rs) * r).astype(BF16)

    uf = _dot(hb, wf_ref[...]).astype(BF16)
    for g in range(FNET_GROUPS):
        zz = _dot(uf[:, g * LANES:(g + 1) * LANES], cs_ref[...])
        z_ref[0, :, g * LANES:(g + 1) * LANES] = zz[:, :LANES]
        z_ref[0, :, FNET_W + g * LANES:FNET_W + (g + 1) * LANES] = zz[:, LANES:]


def _ctx_in_kernel(x_ref, mod_ref, g_ref, wq_ref, wkv_ref, kvlg_ref, wuk_ref, wuv_ref, vone_ref, gk_ref, ones_ref,
                   k_ref, v_ref):
    hb = _norm_mod(x_ref[0], g_ref[...], mod_ref[0, 0:1, :], mod_ref[0, 1:2, :]).astype(BF16)
    krb = _dot(hb, wq_ref[:, Q_LORA:])
    _keys_values(hb, krb, None, wkv_ref, kvlg_ref, wuk_ref, wuv_ref, vone_ref, gk_ref, ones_ref, k_ref, v_ref)


def _even_in_call(x, mod, g, w, tables):
    bsz, n, d = x.shape
    tm = TM_EVEN_IN
    hw = MLA_HEADS * HEAD_PAD
    tok = lambda width: pl.BlockSpec((1, tm, width), lambda b, i: (b, i, 0))
    tab = pl.BlockSpec((tm, LANES), lambda b, i: (i, 0))
    return pl.pallas_call(
        _even_in_kernel,
        grid=(bsz, n // tm),
        in_specs=[
            tok(d),
            pl.BlockSpec((1, 8, d), lambda b, i: (b, 0, 0)),
            _const_spec((1, d)),
            _const_spec(w["wq"].shape), _const_spec(w["wkv"].shape), _const_spec(w["wf"].shape),
            _const_spec((1, Q_LORA)), _const_spec((1, KV_LORA)),
            _const_spec(w["wuq"].shape), _const_spec(w["wuqs"].shape), _const_spec(w["wuk"].shape),
            _const_spec(w["wuv"].shape), _const_spec((1, hw)),
            _const_spec((1, LANES)), _const_spec((1, LANES)), _const_spec((1, LANES)),
            tab, tab,
            _const_spec(tables["cs"].shape), _const_spec(tables["ones"].shape),
        ],
        out_specs=[tok(hw), tok(hw), tok(hw), tok(2 * FNET_W)],
        out_shape=[
            jax.ShapeDtypeStruct((bsz, n, hw), BF16),
            jax.ShapeDtypeStruct((bsz, n, hw), BF16),
            jax.ShapeDtypeStruct((bsz, n, hw), BF16),
            jax.ShapeDtypeStruct((bsz, n, 2 * FNET_W), F32),
        ],
        compiler_params=_params(("parallel", "parallel")),
        name="even_in",
    )(x, mod, g, w["wq"], w["wkv"], w["wf"], w["qlg"], w["kvlg"], w["wuq"], w["wuqs"], w["wuk"], w["wuv"],
      w["vone"], w["gq"], w["gqs"], w["gk"], tables["rc"], tables["rs"], tables["cs"], tables["ones"])


def _ctx_in_call(ctx, mod, g, w, tables):
    bsz, n, d = ctx.shape
    hw = MLA_HEADS * HEAD_PAD
    tok = lambda width: pl.BlockSpec((1, n, width), lambda b: (b, 0, 0))
    return pl.pallas_call(
        _ctx_in_kernel,
        grid=(bsz,),
        in_specs=[
            tok(d),
            pl.BlockSpec((1, 8, d), lambda b: (0, 0, 0)),
            _const_spec((1, d)),
            _const_spec(w["wq"].shape), _const_spec(w["wkv"].shape),
            _const_spec((1, KV_LORA)),
            _const_spec(w["wuk"].shape), _const_spec(w["wuv"].shape), _const_spec((1, hw)),
            _const_spec((1, LANES)), _const_spec(tables["ones"].shape),
        ],
        out_specs=[tok(hw), tok(hw)],
        out_shape=[
            jax.ShapeDtypeStruct((bsz, n, hw), BF16),
            jax.ShapeDtypeStruct((bsz, n, hw), BF16),
        ],
        compiler_params=_params(("parallel",)),
        name="ctx_in",
    )(ctx, mod, g, w["wq"], w["wkv"], w["kvlg"], w["wuk"], w["wuv"], w["vone"], w["gk"], tables["ones"])


DFT_NB = 8


def _seq_dft_kernel(z_ref, are_ref, aim_ref, bre_ref, bim_ref, y_ref, t_scr):
    r = FFT_R
    half = r // DFT_NB
    s = pl.program_id(1)

    @pl.when(s < half)
    def _():
        z = z_ref[0].reshape(r * DFT_NB, 2 * FNET_W).astype(BF16)
        t = _dot(are_ref[...], z[:, :FNET_W]) + _dot(aim_ref[...], z[:, FNET_W:])
        t_scr[:, pl.ds(pl.multiple_of(s * DFT_NB, DFT_NB), DFT_NB), :] = t.reshape(2 * r, DFT_NB, FNET_W)

    @pl.when(s >= half)
    def _():
        j = s - half
        c0 = pl.multiple_of(j * DFT_NB, DFT_NB)
        xr = t_scr[pl.ds(c0, DFT_NB)].reshape(DFT_NB * r, FNET_W).astype(BF16)
        xi = t_scr[pl.ds(r + c0, DFT_NB)].reshape(DFT_NB * r, FNET_W).astype(BF16)
        y = _dot(bre_ref[j], xr) + _dot(bim_ref[j], xi)
        y_ref[0] = y.reshape(y_ref.shape[1:])


def _seq_dft_call(z, tables):
    bsz, n, w2 = z.shape
    r = FFT_R
    nb = DFT_NB
    half = r // nb
    y = pl.pallas_call(
        _seq_dft_kernel,
        grid=(bsz, 2 * half),
        in_specs=[
            pl.BlockSpec((1, r, nb, w2), lambda b, s: (b, 0, jnp.minimum(s, half - 1), 0)),
            _const_spec(tables["a_re"].shape), _const_spec(tables["a_im"].shape),
            _const_spec(tables["b_re"].shape), _const_spec(tables["b_im"].shape),
        ],
        out_specs=pl.BlockSpec((1, r, nb, FNET_W), lambda b, s: (b, 0, jnp.maximum(s - half, 0), 0)),
        out_shape=jax.ShapeDtypeStruct((bsz, r, r, FNET_W), F32),
        scratch_shapes=[pltpu.VMEM((2 * r, r, FNET_W), F32)],
        compiler_params=_params(("parallel", "arbitrary")),
        name="seq_dft",
    )(z.reshape(bsz, r, r, w2), tables["a_re"], tables["a_im"], tables["b_re"], tables["b_im"])
    return y.reshape(bsz, n, FNET_W)


def _attn_kernel(q_ref, kc_ref, k_ref, vc_ref, v_ref, o_ref, s0_ref, s1_ref, m0_ref, m1_ref):
    n_ctx = kc_ref.shape[1]
    n_chunks = k_ref.shape[1] // KC_ATTN
    tq = TQ_ATTN
    n_tiles = q_ref.shape[1] // tq
    s_refs = (s0_ref, s1_ref)
    m_refs = (m0_ref, m1_ref)
    chunks = [(kc_ref, vc_ref, 0, n_ctx, 0)] + [
        (k_ref, v_ref, j * KC_ATTN, KC_ATTN, n_ctx + j * KC_ATTN) for j in range(n_chunks)]

    def lane_groups(x):
        return [x[:, t * LANES:(t + 1) * LANES] for t in range(x.shape[1] // LANES)]

    def segment(i, h_scores, h_values):
        lo = h_scores * HEAD_PAD
        q = q_ref[0, pl.ds(pl.multiple_of(i * tq, tq), tq), lo:lo + HEAD_PAD]
        s_out = s_refs[h_scores]
        m_run = acc = None
        if h_values is not None:
            s_in = s_refs[h_values]
            m_in = m_refs[h_values][...]
            lv = h_values * HEAD_PAD
        for kr, vr, r0, width, c0 in chunks:
            if h_values is not None:
                p = [jnp.exp2(g - m_in) for g in lane_groups(s_in[:, c0:c0 + width])]
                pv = _dot(jnp.concatenate(p, axis=1).astype(BF16), vr[0, r0:r0 + width, lv:lv + HEAD_PAD])
                acc = pv if acc is None else acc + pv
            s = _dot_nt(q, kr[0, r0:r0 + width, lo:lo + HEAD_PAD])
            s_out[:, c0:c0 + width] = s
            m_run = functools.reduce(jnp.maximum, lane_groups(s) if m_run is None else [m_run] + lane_groups(s))
        m_refs[h_scores][...] = jnp.broadcast_to(jnp.max(m_run, axis=-1, keepdims=True), (tq, LANES))
        return acc

    segment(0, 0, None)

    def body(i, carry):
        acc0 = segment(i, 1, 0)
        acc1 = segment(jnp.minimum(i + 1, n_tiles - 1), 0, 1)
        out0 = acc0 / acc0[:, V_HEAD:V_HEAD + 1]
        out1 = acc1 / acc1[:, 0:1]
        lane = lax.broadcasted_iota(jnp.int32, out0.shape, 1)
        o_ref[0, pl.ds(pl.multiple_of(i * tq, tq), tq), :] = jnp.where(lane < V_HEAD, out0, out1).astype(BF16)
        return carry

    lax.fori_loop(0, n_tiles, body, 0, unroll=4)


def _attn_call(q, k_ctx, k, v_ctx, v):
    bsz, n, hw = q.shape
    n_ctx = k_ctx.shape[1]
    tq = TQ_ATTN
    pair = 2 * HEAD_PAD
    return pl.pallas_call(
        _attn_kernel,
        grid=(bsz, MLA_HEADS // 2),
        in_specs=[
            pl.BlockSpec((1, n, pair), lambda b, h: (b, 0, h)),
            pl.BlockSpec((1, n_ctx, pair), lambda b, h: (b, 0, h)),
            pl.BlockSpec((1, n, pair), lambda b, h: (b, 0, h)),
            pl.BlockSpec((1, n_ctx, pair), lambda b, h: (b, 0, h)),
            pl.BlockSpec((1, n, pair), lambda b, h: (b, 0, h)),
        ],
        out_specs=pl.BlockSpec((1, n, 2 * V_HEAD), lambda b, h: (b, 0, h)),
        out_shape=jax.ShapeDtypeStruct((bsz, n, MLA_HEADS * V_HEAD), BF16),
        scratch_shapes=[pltpu.VMEM((tq, n_ctx + n), F32), pltpu.VMEM((tq, n_ctx + n), F32),
                        pltpu.VMEM((tq, LANES), F32), pltpu.VMEM((tq, LANES), F32)],
        compiler_params=_params(("parallel", "parallel")),
        name="attention",
    )(q, k_ctx, k, v_ctx, v)


def _ffn_tail(x1, mod_ref, g2n_ref, w1_ref, w3_ref, w2_ref, out_ref):
    h2 = _norm_mod(x1, g2n_ref[...], mod_ref[0, 3:4, :], mod_ref[0, 4:5, :]).astype(BF16)
    acc = None
    for c0, cw in FF_CHUNKS:
        a = _dot(h2, w1_ref[:, c0:c0 + cw])
        b = _dot(h2, w3_ref[:, c0:c0 + cw])
        part = _dot((_silu(a) * b).astype(BF16), w2_ref[c0:c0 + cw, :])
        acc = part if acc is None else acc + part
    out_ref[0] = x1 + mod_ref[0, 5:6, :] * acc


def _even_out_kernel(x_ref, o_ref, y_ref, mod_ref, g2n_ref, wo_ref, w1_ref, w3_ref, w2_ref, out_ref):
    hv = MLA_HEADS * V_HEAD
    mix = _dot(o_ref[0], wo_ref[0:hv, :]) + _dot(y_ref[0].astype(BF16), wo_ref[hv:, :])
    x1 = x_ref[0] + mod_ref[0, 2:3, :] * mix
    _ffn_tail(x1, mod_ref, g2n_ref, w1_ref, w3_ref, w2_ref, out_ref)


def _even_out_call(x, o, y, mod, g2n, wo, w1, w3, w2):
    bsz, n, d = x.shape
    tm = TM_OUT
    tok = lambda width: pl.BlockSpec((1, tm, width), lambda b, i: (b, i, 0))
    return pl.pallas_call(
        _even_out_kernel,
        grid=(bsz, n // tm),
        in_specs=[
            tok(d), tok(o.shape[-1]), tok(y.shape[-1]),
            pl.BlockSpec((1, 8, d), lambda b, i: (b, 0, 0)),
            _const_spec((1, d)),
            _const_spec(wo.shape), _const_spec(w1.shape), _const_spec(w3.shape), _const_spec(w2.shape),
        ],
        out_specs=tok(d),
        out_shape=jax.ShapeDtypeStruct((bsz, n, d), F32),
        compiler_params=_params(("parallel", "parallel")),
        name="even_out_ffn",
    )(x, o, y, mod, g2n, wo, w1, w3, w2)


CONV_RB = 64


SUBLANES = 8


def _conf_conv_block(cbuf, dw_ref, bias, r0, ls):
    rows = CONV_RB + SUBLANES
    acc = jnp.broadcast_to(bias, (CONV_RB, LANES))
    for r in range(SUBLANES):
        part = None
        for o in range(r, CONF_WIDTH + 1, SUBLANES):
            if o == 0:
                continue
            term = dw_ref[o - 1:o, ls] * cbuf[r0 + o - r:r0 + o - r + rows, ls]
            part = term if part is None else part + term
        acc = acc + part[r:r + CONV_RB]
    return acc


def _zero_row(val_row, zero_ref):
    bits = lax.bitcast_convert_type(val_row, jnp.int32) & zero_ref[...]
    return lax.bitcast_convert_type(bits, F32)


def _odd_kernel(xp_ref, x_ref, xn_ref, xlag_ref, mod_ref, modlag_ref, zero_ref, g1n_ref, win_ref, dw_ref, dwb_ref,
                lng_ref, lnb_ref, scw_ref, wout_ref, g2n_ref, w1_ref, w3_ref, w2_ref, out_ref,
                hb_ref, cbuf, sbuf, bbuf, mbuf, acc_ref, *, tiles_per_seq):
    tm = x_ref.shape[1]
    t = pl.program_id(0)

    @pl.when(t == 0)
    def _():
        mbuf[...] = jnp.zeros(mbuf.shape, mbuf.dtype)

    mix = _dot(mbuf[...], wout_ref[...])
    x1 = xlag_ref[0] + modlag_ref[0, 2:3, :] * mix
    h2 = _norm_mod(x1, g2n_ref[...], modlag_ref[0, 3:4, :], modlag_ref[0, 4:5, :]).astype(BF16)

    i = lax.rem(jnp.minimum(t, pl.num_programs(0) - 2), tiles_per_seq)
    last = tiles_per_seq - 1
    g = g1n_ref[...]
    sh = mod_ref[0, 0:1, :]
    sc = mod_ref[0, 1:2, :]
    hb_ref[0:HALO, :] = _norm_mod(xp_ref[0], g, sh, sc).astype(BF16)
    hb_ref[HALO:HALO + tm, :] = _norm_mod(x_ref[0], g, sh, sc).astype(BF16)
    hb_ref[HALO + tm:, :] = _norm_mod(xn_ref[0], g, sh, sc).astype(BF16)
    u = _dot(hb_ref[...], win_ref[...])

    conf = u[:, 0:CONF_W] * jax.nn.sigmoid(u[:, CONF_W:2 * CONF_W])
    cx = u[:, 2 * CONF_W + SC_W:2 * CONF_W + 2 * SC_W] * u[:, 2 * CONF_W + 2 * SC_W:]
    keep_lo = i > 0
    keep_hi = i < last
    cbuf[0:HALO, :] = jnp.where(keep_lo, conf[0:HALO], 0.0)
    cbuf[HALO:HALO + tm, :] = conf[HALO:HALO + tm]
    cbuf[HALO + tm:, :] = jnp.where(keep_hi, conf[HALO + tm:], 0.0)
    sbuf[0:HALO, :] = jnp.where(keep_lo, cx[0:HALO], 0.0)
    sbuf[HALO:HALO + tm, :] = cx[HALO:HALO + tm]
    sbuf[HALO + tm:, :] = jnp.where(keep_hi, cx[HALO + tm:], 0.0)
    bbuf[...] = u[HALO:HALO + tm, 2 * CONF_W:2 * CONF_W + SC_W]

    assert HALO == (CONF_WIDTH - 1) // 2 + 1
    pad_s = (SC_WIDTH - 1) // 2

    def conv_block(r0, lg, order_after):
        ls = slice(lg * LANES, (lg + 1) * LANES)
        bias = dwb_ref[:, ls]
        if order_after is not None:
            bias = bias + order_after
        acc = _conf_conv_block(cbuf, dw_ref, bias, r0, ls)
        mu = jnp.mean(acc, axis=-1, keepdims=True)
        cen = acc - mu
        var = jnp.mean(cen * cen, axis=-1, keepdims=True)
        yn = cen * lax.rsqrt(var + EPS) * lng_ref[:, ls] + lnb_ref[:, ls]
        mbuf[r0:r0 + CONV_RB, ls] = _silu(yn).astype(BF16)

        acc = None
        for j in range(SC_WIDTH):
            o = HALO + r0 + j - pad_s
            term = scw_ref[j:j + 1, ls] * sbuf[o:o + CONV_RB, ls]
            acc = term if acc is None else acc + term
        mbuf[r0:r0 + CONV_RB, CONF_W + lg * LANES:CONF_W + (lg + 1) * LANES] = (
            bbuf[r0:r0 + CONV_RB, ls] * acc).astype(BF16)
        return yn[0:1, :]

    blocks = [(r0, lg) for r0 in range(0, tm, CONV_RB) for lg in range(CONF_W // LANES)]
    n_groups = len(FF_CHUNKS) - 1
    after_conv = None
    for k, (c0, cw) in enumerate(FF_CHUNKS):
        a = _dot(h2, w1_ref[:, c0:c0 + cw])
        b = _dot(h2, w3_ref[:, c0:c0 + cw])
        part = _dot((_silu(a) * b).astype(BF16), w2_ref[c0:c0 + cw, :])
        if k == 0:
            acc_ref[...] = part
        else:
            acc_ref[...] += part
        if after_conv is not None:
            acc_ref[0:1, 0:LANES] += after_conv
            after_conv = None
        if k < n_groups:
            after_chunk = _zero_row(part[0:1, 0:LANES], zero_ref)
            tail = None
            for r0, lg in blocks[k * len(blocks) // n_groups:(k + 1) * len(blocks) // n_groups]:
                tail = conv_block(r0, lg, after_chunk)
            after_conv = _zero_row(tail, zero_ref)
    out_ref[0] = x1 + modlag_ref[0, 5:6, :] * acc_ref[...]


def _odd_call(x, mod, g1n, win, dw, dwb, lng, lnb, scw, wout, g2n, w1, w3, w2):
    bsz, n, d = x.shape
    tm = TM_ODD
    hb = tm // HALO
    nh = n // HALO
    nt = n // tm
    total = bsz * nt

    def cur(t):
        u = jnp.minimum(t, total - 1)
        return u // nt, u % nt

    def lag(t):
        u = jnp.maximum(t - 1, 0)
        return u // nt, u % nt

    def halo_lo(t):
        b, i = cur(t)
        return b, jnp.maximum(i * hb - 1, 0), 0

    def halo_hi(t):
        b, i = cur(t)
        return b, jnp.minimum((i + 1) * hb, nh - 1), 0

    return pl.pallas_call(
        functools.partial(_odd_kernel, tiles_per_seq=nt),
        grid=(total + 1,),
        in_specs=[
            pl.BlockSpec((1, HALO, d), halo_lo),
            pl.BlockSpec((1, tm, d), lambda t: (*cur(t), 0)),
            pl.BlockSpec((1, HALO, d), halo_hi),
            pl.BlockSpec((1, tm, d), lambda t: (*lag(t), 0)),
            pl.BlockSpec((1, 8, d), lambda t: (cur(t)[0], 0, 0)),
            pl.BlockSpec((1, 8, d), lambda t: (lag(t)[0], 0, 0)),
            _const_spec((1, LANES)),
            _const_spec((1, d)),
            _const_spec(win.shape),
            _const_spec(dw.shape), _const_spec(dwb.shape), _const_spec(lng.shape), _const_spec(lnb.shape),
            _const_spec(scw.shape),
            _const_spec(wout.shape),
            _const_spec((1, d)),
            _const_spec(w1.shape), _const_spec(w3.shape), _const_spec(w2.shape),
        ],
        out_specs=pl.BlockSpec((1, tm, d), lambda t: (*lag(t), 0)),
        out_shape=jax.ShapeDtypeStruct((bsz, n, d), F32),
        scratch_shapes=[
            pltpu.VMEM((tm + 2 * HALO, d), BF16),
            pltpu.VMEM((tm + 2 * HALO, CONF_W), F32),
            pltpu.VMEM((tm + 2 * HALO, SC_W), F32),
            pltpu.VMEM((tm, SC_W), F32),
            pltpu.VMEM((tm, CONF_W + SC_W), BF16),
            pltpu.VMEM((tm, d), F32),
        ],
        compiler_params=_params(("arbitrary",)),
        name="odd_layer",
    )(x, x, x, x, mod, mod, jnp.zeros((1, LANES), jnp.int32), g1n, win, dw, dwb, lng, lnb, scw, wout, g2n,
      w1, w3, w2)


def _dft_tables():
    r = FFT_R
    gw = FNET_GROUP_W
    idx = np.arange(gw)
    ang = 2.0 * np.pi * ((idx[:, None] * idx[None, :]) % gw) / gw
    cs = np.concatenate([np.cos(ang), -np.sin(ang)], axis=1) / np.sqrt(gw)
    a = np.arange(r)
    nb = DFT_NB
    eye = np.eye(nb)
    ang1 = 2.0 * np.pi * ((a[:, None] * a[None, :]) % r) / r
    fr, fi = np.cos(ang1) / 8.0, -np.sin(ang1) / 8.0
    a_re = np.concatenate([np.kron(fr, eye), np.kron(fi, eye)], axis=0)
    a_im = np.concatenate([np.kron(-fi, eye), np.kron(fr, eye)], axis=0)
    n = r * r
    c = a[:, None, None]
    dd = a[None, :, None]
    bb = a[None, None, :]
    ang2 = 2.0 * np.pi * ((bb * (c + r * dd)) % n) / n
    gr, gi_neg = np.cos(ang2) / 8.0, np.sin(ang2) / 8.0
    b_re = np.zeros((r // nb, r, nb, nb, r))
    b_im = np.zeros((r // nb, r, nb, nb, r))
    for ci in range(nb):
        b_re[:, :, ci, ci, :] = gr[ci::nb]
        b_im[:, :, ci, ci, :] = gi_neg[ci::nb]
    b_re = b_re.reshape(r // nb, r * nb, nb * r)
    b_im = b_im.reshape(r // nb, r * nb, nb * r)
    named = (("cs", cs), ("a_re", a_re), ("a_im", a_im), ("b_re", b_re), ("b_im", b_im))
    return {name: jnp.asarray(t, F32).astype(BF16) for name, t in named}


def _rope_tables(n):
    rows = n // GRID_W
    row = jnp.broadcast_to(jnp.arange(rows, dtype=F32)[:, None], (rows, GRID_W)).reshape(n)
    col = jnp.broadcast_to(jnp.arange(GRID_W, dtype=F32)[None, :], (rows, GRID_W)).reshape(n)
    per_axis = QK_ROPE // 4
    inv_freq = ROPE_BASE ** (-jnp.arange(per_axis, dtype=F32) / per_axis)
    ang = jnp.concatenate([row[:, None] * inv_freq, col[:, None] * inv_freq], axis=-1)
    cos, sin = jnp.cos(ang), jnp.sin(ang)
    tail = LANES - QK_HEAD
    rc = jnp.concatenate([jnp.ones((n, QK_NOPE), F32), cos, cos, jnp.ones((n, tail), F32)], axis=-1)
    rs = jnp.concatenate([jnp.zeros((n, QK_NOPE), F32), -sin, sin, jnp.zeros((n, tail), F32)], axis=-1)
    return rc, rs


def _swap_rope_halves(t):
    r1 = t[..., QK_NOPE:QK_NOPE + ROPE_HALF]
    r2 = t[..., QK_NOPE + ROPE_HALF:]
    return jnp.concatenate([jnp.zeros_like(t[..., :QK_NOPE]), r2, r1], axis=-1)


def _pad_heads(w, width):
    r = w.shape[0]
    return jnp.pad(w, ((0, 0), (0, 0), (0, HEAD_PAD - width))).reshape(r, MLA_HEADS * HEAD_PAD)


def _pad_gain(g):
    return jnp.pad(g, (0, HEAD_PAD - QK_HEAD)).reshape(1, HEAD_PAD)


def _pad_values(w):
    r = w.shape[0]
    pairs = w.reshape(r, MLA_HEADS // 2, 2, V_HEAD)
    zero = jnp.zeros_like(pairs[:, :, 0])
    even = jnp.concatenate([pairs[:, :, 0], zero], axis=-1)
    odd = jnp.concatenate([zero, pairs[:, :, 1]], axis=-1)
    return jnp.stack([even, odd], axis=2).reshape(r, MLA_HEADS * HEAD_PAD)


def _value_ones_row():
    row = np.zeros((MLA_HEADS, HEAD_PAD), np.float32)
    row[0::2, V_HEAD] = 1.0
    row[1::2, 0] = 1.0
    return jnp.asarray(row.reshape(1, MLA_HEADS * HEAD_PAD))


def kernel(x, c, ctx, c_ctx, ada_w, ada_b, norm1_g, norm2_g, ffn_w1, ffn_w3, ffn_w2, a_w_in, a_q_ln_g, a_kv_ln_g, a_w_uq, a_w_uk, a_w_uv, a_q_norm_g, a_k_norm_g, a_w_out, b_w_in, b_conf_dw, b_conf_dw_b, b_conf_ln_g, b_conf_ln_b, b_sc_dw, b_w_out):
    bsz, n, d = x.shape
    depth = ada_w.shape[0]
    assert depth == 2 and d == D_MODEL and n == FFT_R * FFT_R and ctx.shape[1] == CTX_LEN

    rows = 16
    cvec = jnp.concatenate([c, c_ctx[None, :], jnp.zeros((rows - bsz - 1, d), F32)], axis=0)
    ada = _ada_call(cvec, ada_w, ada_b).reshape(depth, rows, 6, d)
    unit = jnp.array([0.0, 1.0, 0.0, 0.0, 1.0, 0.0], F32)[None, None, :, None]
    mods = jnp.pad(ada + unit, ((0, 0), (0, 0), (0, 2), (0, 0)))

    tables = _dft_tables()
    tables["rc"], tables["rs"] = _rope_tables(n)
    tables["ones"] = jnp.ones((LANES, LANES), BF16)

    w_in = a_w_in[0]
    kr_cols = jnp.pad(w_in[:, Q_LORA + KV_LORA:EVEN_KV_END], ((0, 0), (QK_NOPE, LANES - QK_HEAD)))
    w = {
        "wq": jnp.concatenate([w_in[:, :Q_LORA], kr_cols], axis=1).astype(BF16),
        "wkv": w_in[:, Q_LORA:Q_LORA + KV_LORA].astype(BF16),
        "wf": w_in[:, EVEN_KV_END:].astype(BF16),
        "qlg": a_q_ln_g[0].reshape(1, Q_LORA),
        "kvlg": a_kv_ln_g[0].reshape(1, KV_LORA),
        "wuq": _pad_heads(a_w_uq[0], QK_HEAD).astype(BF16),
        "wuqs": _pad_heads(_swap_rope_halves(a_w_uq[0]), QK_HEAD).astype(BF16),
        "wuk": _pad_heads(a_w_uk[0], QK_NOPE).astype(BF16),
        "wuv": _pad_values(a_w_uv[0]).astype(BF16),
        "vone": _value_ones_row(),
        "gq": _pad_gain(a_q_norm_g[0]),
        "gqs": _pad_gain(_swap_rope_halves(a_q_norm_g[0])),
        "gk": _pad_gain(a_k_norm_g[0]),
    }
    g1 = norm1_g[0].reshape(1, d)
    q, k, v, z = _even_in_call(x, mods[0, :bsz], g1, w, tables)
    k_ctx, v_ctx = _ctx_in_call(ctx, mods[0, bsz:bsz + 1], g1, w, tables)
    y = _seq_dft_call(z, tables)
    o = _attn_call(q, k_ctx, k, v_ctx, v)
    x = _even_out_call(x, o, y, mods[0, :bsz], norm2_g[0].reshape(1, d), a_w_out[0].astype(BF16),
                       ffn_w1[0].astype(BF16), ffn_w3[0].astype(BF16), ffn_w2[0].astype(BF16))

    x = _odd_call(x, mods[1, :bsz], norm1_g[1].reshape(1, d), b_w_in[0].astype(BF16),
                  b_conf_dw[0], b_conf_dw_b[0].reshape(1, CONF_W), b_conf_ln_g[0].reshape(1, CONF_W),
                  b_conf_ln_b[0].reshape(1, CONF_W), b_sc_dw[0], b_w_out[0].astype(BF16),
                  norm2_g[1].reshape(1, d), ffn_w1[1].astype(BF16), ffn_w3[1].astype(BF16),
                  ffn_w2[1].astype(BF16))
    return x
```

```python
import functools
import math

import numpy as np
import jax
import jax.numpy as jnp
from jax import lax
from jax.experimental import pallas as pl
from jax.experimental.pallas import tpu as pltpu

F32 = jnp.float32
BF16 = jnp.bfloat16

D_MODEL = 1024
CTX_LEN = 256
GRID_W = 64
EPS = 1e-6
MLA_HEADS = 8
QK_NOPE = 64
QK_ROPE = 32
QK_HEAD = QK_NOPE + QK_ROPE
V_HEAD = 64
Q_LORA = 384
KV_LORA = 256
ROPE_BASE = 10000.0
FNET_GROUPS = 4
FNET_GROUP_W = 128
FNET_W = FNET_GROUPS * FNET_GROUP_W
CONF_GROUPS = 4
CONF_W = 512
CONF_WIDTH = 31
SC_W = 512
SC_WIDTH = 3
D_FF = 2816
EVEN_KV_END = Q_LORA + KV_LORA + QK_ROPE

LANES = 128
HEAD_PAD = LANES
ROPE_HALF = QK_ROPE // 2
FFT_R = 64
HALO = 16
VMEM_LIMIT = 56 * 1024 * 1024

TM_EVEN_IN = 512
TQ_ATTN = 512
KC_ATTN = 512
TM_OUT = 512
TM_ODD = 512
FF_CHUNKS = tuple((c, min(512, D_FF - c)) for c in range(0, D_FF, 512))

Q_SCALE = QK_HEAD ** -0.5 * math.log2(math.e)


def _dot(a, b):
    return jnp.dot(a, b, preferred_element_type=F32)


def _dot_nt(a, b):
    return lax.dot_general(a, b, (((1,), (1,)), ((), ())), preferred_element_type=F32)


def _inv_rms(x, n):
    return lax.rsqrt(jnp.sum(x * x, axis=-1, keepdims=True) * (1.0 / n) + EPS)


def _norm_mod(x, g, shift, scale1p):
    return (x * _inv_rms(x, x.shape[-1]) * g) * scale1p + shift


def _silu(a):
    return a * jax.nn.sigmoid(a)


def _const_spec(shape):
    nd = len(shape)
    return pl.BlockSpec(shape, lambda *_: (0,) * nd, pipeline_mode=pl.Buffered(1))


def _params(sem, flags=None):
    return pltpu.CompilerParams(dimension_semantics=sem, vmem_limit_bytes=VMEM_LIMIT, flags=flags)


def _ada_kernel(c_ref, w_ref, b_ref, o_ref):
    s = _silu(c_ref[...]).astype(BF16)
    o_ref[0] = _dot(s, w_ref[0].astype(BF16)) + b_ref[0]


def _ada_call(cvec, ada_w, ada_b):
    depth, d, n6 = ada_w.shape
    rows = cvec.shape[0]
    tn = 1536
    return pl.pallas_call(
        _ada_kernel,
        grid=(depth, n6 // tn),
        in_specs=[
            pl.BlockSpec((rows, d), lambda l, j: (0, 0)),
            pl.BlockSpec((1, d, tn), lambda l, j: (l, 0, j)),
            pl.BlockSpec((1, 1, tn), lambda l, j: (l, 0, j)),
        ],
        out_specs=pl.BlockSpec((1, rows, tn), lambda l, j: (l, 0, j)),
        out_shape=jax.ShapeDtypeStruct((depth, rows, n6), F32),
        compiler_params=_params(("arbitrary", "arbitrary")),
        name="ada_mod",
    )(cvec, ada_w, ada_b.reshape(depth, 1, n6))


def _head_inv_rms(t, ones):
    ss = _dot((t * t).astype(BF16), ones)
    return lax.rsqrt(ss * (1.0 / QK_HEAD) + EPS)


def _keys_values(hb, krb, rope, wkv_ref, kvlg_ref, wuk_ref, wuv_ref, vone_ref, gk_ref, ones_ref, k_ref, v_ref):
    ckv = _dot(hb, wkv_ref[...])
    ckvn = (ckv * _inv_rms(ckv, KV_LORA) * kvlg_ref[...]).astype(BF16)
    kf = _dot(ckvn, wuk_ref[...])
    v_ref[0] = (_dot(ckvn, wuv_ref[...]) + vone_ref[...]).astype(BF16)
    gk = gk_ref[...]
    krg = krb * gk
    if rope is not None:
        rc, rs = rope
        lane = lax.broadcasted_iota(jnp.int32, krg.shape, 1)
        partner = jnp.where(lane < QK_NOPE + ROPE_HALF, pltpu.roll(krg, LANES - ROPE_HALF, 1),
                            pltpu.roll(krg, ROPE_HALF, 1))
        krg = krg * rc + partner * rs
    ones = ones_ref[...]
    for h in range(MLA_HEADS):
        kn = kf[:, h * HEAD_PAD:(h + 1) * HEAD_PAD]
        r = _head_inv_rms(kn + krb, ones)
        k_ref[0, :, h * HEAD_PAD:(h + 1) * HEAD_PAD] = ((kn * gk + krg) * r).astype(BF16)


def _even_in_kernel(x_ref, mod_ref, g_ref, wq_ref, wkv_ref, wf_ref, qlg_ref, kvlg_ref, wuq_ref, wuqs_ref, wuk_ref,
                    wuv_ref, vone_ref, gq_ref, gqs_ref, gk_ref, rc_ref, rs_ref, cs_ref, ones_ref,
                    q_ref, k_ref, v_ref, z_ref):
    hb = _norm_mod(x_ref[0], g_ref[...], mod_ref[0, 0:1, :], mod_ref[0, 1:2, :]).astype(BF16)
    rc, rs = rc_ref[...], rs_ref[...]
    uq = _dot(hb, wq_ref[...])
    _keys_values(hb, uq[:, Q_LORA:], (rc, rs), wkv_ref, kvlg_ref, wuk_ref, wuv_ref, vone_ref, gk_ref, ones_ref,
                 k_ref, v_ref)

    cq = uq[:, :Q_LORA]
    cqn = (cq * _inv_rms(cq, Q_LORA) * qlg_ref[...]).astype(BF16)
    qf = _dot(cqn, wuq_ref[...])
    qs = _dot(cqn, wuqs_ref[...])
    gq = gq_ref[...] * Q_SCALE
    gqs = gqs_ref[...] * Q_SCALE
    ones = ones_ref[...]
    for h in range(MLA_HEADS):
        sl = slice(h * HEAD_PAD, (h + 1) * HEAD_PAD)
        qh = qf[:, sl]
        r = _head_inv_rms(qh, ones)
        q_ref[0, :, sl] = ((qh * gq * rc + qs[:, sl] * gqs * rs) * r).astype(BF16)

    uf = _dot(hb, wf_ref[...]).astype(BF16)
    for g in range(FNET_GROUPS):
        zz = _dot(uf[:, g * LANES:(g + 1) * LANES], cs_ref[...])
        z_ref[0, :, g * LANES:(g + 1) * LANES] = zz[:, :LANES]
        z_ref[0, :, FNET_W + g * LANES:FNET_W + (g + 1) * LANES] = zz[:, LANES:]


def _ctx_in_kernel(x_ref, mod_ref, g_ref, wq_ref, wkv_ref, kvlg_ref, wuk_ref, wuv_ref, vone_ref, gk_ref, ones_ref,
                   k_ref, v_ref):
    hb = _norm_mod(x_ref[0], g_ref[...], mod_ref[0, 0:1, :], mod_ref[0, 1:2, :]).astype(BF16)
    krb = _dot(hb, wq_ref[:, Q_LORA:])
    _keys_values(hb, krb, None, wkv_ref, kvlg_ref, wuk_ref, wuv_ref, vone_ref, gk_ref, ones_ref, k_ref, v_ref)


def _even_in_call(x, mod, g, w, tables):
    bsz, n, d = x.shape
    tm = TM_EVEN_IN
    hw = MLA_HEADS * HEAD_PAD
    tok = lambda width: pl.BlockSpec((1, tm, width), lambda b, i: (b, i, 0))
    tab = pl.BlockSpec((tm, LANES), lambda b, i: (i, 0))
    return pl.pallas_call(
        _even_in_kernel,
        grid=(bsz, n // tm),
        in_specs=[
            tok(d),
            pl.BlockSpec((1, 8, d), lambda b, i: (b, 0, 0)),
            _const_spec((1, d)),
            _const_spec(w["wq"].shape), _const_spec(w["wkv"].shape), _const_spec(w["wf"].shape),
            _const_spec((1, Q_LORA)), _const_spec((1, KV_LORA)),
            _const_spec(w["wuq"].shape), _const_spec(w["wuqs"].shape), _const_spec(w["wuk"].shape),
            _const_spec(w["wuv"].shape), _const_spec((1, hw)),
            _const_spec((1, LANES)), _const_spec((1, LANES)), _const_spec((1, LANES)),
            tab, tab,
            _const_spec(tables["cs"].shape), _const_spec(tables["ones"].shape),
        ],
        out_specs=[tok(hw), tok(hw), tok(hw), tok(2 * FNET_W)],
        out_shape=[
            jax.ShapeDtypeStruct((bsz, n, hw), BF16),
            jax.ShapeDtypeStruct((bsz, n, hw), BF16),
            jax.ShapeDtypeStruct((bsz, n, hw), BF16),
            jax.ShapeDtypeStruct((bsz, n, 2 * FNET_W), F32),
        ],
        compiler_params=_params(("parallel", "parallel")),
        name="even_in",
    )(x, mod, g, w["wq"], w["wkv"], w["wf"], w["qlg"], w["kvlg"], w["wuq"], w["wuqs"], w["wuk"], w["wuv"],
      w["vone"], w["gq"], w["gqs"], w["gk"], tables["rc"], tables["rs"], tables["cs"], tables["ones"])


def _ctx_in_call(ctx, mod, g, w, tables):
    bsz, n, d = ctx.shape
    hw = MLA_HEADS * HEAD_PAD
    tok = lambda width: pl.BlockSpec((1, n, width), lambda b: (b, 0, 0))
    return pl.pallas_call(
        _ctx_in_kernel,
        grid=(bsz,),
        in_specs=[
            tok(d),
            pl.BlockSpec((1, 8, d), lambda b: (0, 0, 0)),
            _const_spec((1, d)),
            _const_spec(w["wq"].shape), _const_spec(w["wkv"].shape),
            _const_spec((1, KV_LORA)),
            _const_spec(w["wuk"].shape), _const_spec(w["wuv"].shape), _const_spec((1, hw)),
            _const_spec((1, LANES)), _const_spec(tables["ones"].shape),
        ],
        out_specs=[tok(hw), tok(hw)],
        out_shape=[
            jax.ShapeDtypeStruct((bsz, n, hw), BF16),
            jax.ShapeDtypeStruct((bsz, n, hw), BF16),
        ],
        compiler_params=_params(("parallel",)),
        name="ctx_in",
    )(ctx, mod, g, w["wq"], w["wkv"], w["kvlg"], w["wuk"], w["wuv"], w["vone"], w["gk"], tables["ones"])


DFT_NB = 8
DFT_GROUPS = 2


def _seq_dft_kernel(z_ref, are_ref, aim_ref, bre_ref, bim_ref, y_ref, t_scr):
    r = FFT_R
    half = r // (DFT_NB * DFT_GROUPS)
    s = pl.program_id(1)

    @pl.when(s < half)
    def _():
        for g in range(DFT_GROUPS):
            lo = g * DFT_NB
            z = z_ref[0, :, lo:lo + DFT_NB, :].reshape(r * DFT_NB, 2 * FNET_W).astype(BF16)
            t = _dot(are_ref[...], z[:, :FNET_W]) + _dot(aim_ref[...], z[:, FNET_W:])
            b0 = pl.multiple_of(s * (DFT_NB * DFT_GROUPS) + lo, DFT_NB)
            t_scr[:, pl.ds(b0, DFT_NB), :] = t.reshape(2 * r, DFT_NB, FNET_W)

    @pl.when(s >= half)
    def _():
        for g in range(DFT_GROUPS):
            lo = g * DFT_NB
            j = (s - half) * DFT_GROUPS + g
            c0 = pl.multiple_of(j * DFT_NB, DFT_NB)
            xr = t_scr[pl.ds(c0, DFT_NB)].reshape(DFT_NB * r, FNET_W).astype(BF16)
            xi = t_scr[pl.ds(r + c0, DFT_NB)].reshape(DFT_NB * r, FNET_W).astype(BF16)
            y = _dot(bre_ref[j], xr) + _dot(bim_ref[j], xi)
            y_ref[0, :, lo:lo + DFT_NB, :] = y.reshape(r, DFT_NB, FNET_W)


def _seq_dft_call(z, tables):
    bsz, n, w2 = z.shape
    r = FFT_R
    nb = DFT_NB * DFT_GROUPS
    half = r // nb
    y = pl.pallas_call(
        _seq_dft_kernel,
        grid=(bsz, 2 * half),
        in_specs=[
            pl.BlockSpec((1, r, nb, w2), lambda b, s: (b, 0, jnp.minimum(s, half - 1), 0)),
            _const_spec(tables["a_re"].shape), _const_spec(tables["a_im"].shape),
            _const_spec(tables["b_re"].shape), _const_spec(tables["b_im"].shape),
        ],
        out_specs=pl.BlockSpec((1, r, nb, FNET_W), lambda b, s: (b, 0, jnp.maximum(s - half, 0), 0)),
        out_shape=jax.ShapeDtypeStruct((bsz, r, r, FNET_W), F32),
        scratch_shapes=[pltpu.VMEM((2 * r, r, FNET_W), F32)],
        compiler_params=_params(("parallel", "arbitrary")),
        name="seq_dft",
    )(z.reshape(bsz, r, r, w2), tables["a_re"], tables["a_im"], tables["b_re"], tables["b_im"])
    return y.reshape(bsz, n, FNET_W)


def _attn_kernel(q_ref, kc_ref, k_ref, vc_ref, v_ref, o_ref, s0_ref, s1_ref, m0_ref, m1_ref):
    n_ctx = kc_ref.shape[1]
    n_chunks = k_ref.shape[1] // KC_ATTN
    tq = TQ_ATTN
    n_tiles = q_ref.shape[1] // tq
    s_refs = (s0_ref, s1_ref)
    m_refs = (m0_ref, m1_ref)
    chunks = [(kc_ref, vc_ref, 0, n_ctx, 0)] + [
        (k_ref, v_ref, j * KC_ATTN, KC_ATTN, n_ctx + j * KC_ATTN) for j in range(n_chunks)]

    def lane_groups(x):
        return [x[:, t * LANES:(t + 1) * LANES] for t in range(x.shape[1] // LANES)]

    def segment(i, h_scores, h_values):
        lo = h_scores * HEAD_PAD
        q = q_ref[0, pl.ds(pl.multiple_of(i * tq, tq), tq), lo:lo + HEAD_PAD]
        s_out = s_refs[h_scores]
        m_run = acc = None
        if h_values is not None:
            s_in = s_refs[h_values]
            m_in = m_refs[h_values][...]
            lv = h_values * HEAD_PAD
        for kr, vr, r0, width, c0 in chunks:
            if h_values is not None:
                p = [jnp.exp2(g - m_in) for g in lane_groups(s_in[:, c0:c0 + width])]
                pv = _dot(jnp.concatenate(p, axis=1).astype(BF16), vr[0, r0:r0 + width, lv:lv + HEAD_PAD])
                acc = pv if acc is None else acc + pv
            s = _dot_nt(q, kr[0, r0:r0 + width, lo:lo + HEAD_PAD])
            s_out[:, c0:c0 + width] = s
            m_run = functools.reduce(jnp.maximum, lane_groups(s) if m_run is None else [m_run] + lane_groups(s))
        m_refs[h_scores][...] = jnp.broadcast_to(jnp.max(m_run, axis=-1, keepdims=True), (tq, LANES))
        return acc

    segment(0, 0, None)

    def body(i, carry):
        acc0 = segment(i, 1, 0)
        acc1 = segment(jnp.minimum(i + 1, n_tiles - 1), 0, 1)
        out0 = acc0 / acc0[:, V_HEAD:V_HEAD + 1]
        out1 = acc1 / acc1[:, 0:1]
        lane = lax.broadcasted_iota(jnp.int32, out0.shape, 1)
        o_ref[0, pl.ds(pl.multiple_of(i * tq, tq), tq), :] = jnp.where(lane < V_HEAD, out0, out1).astype(BF16)
        return carry

    lax.fori_loop(0, n_tiles, body, 0, unroll=4)


def _attn_call(q, k_ctx, k, v_ctx, v):
    bsz, n, hw = q.shape
    n_ctx = k_ctx.shape[1]
    tq = TQ_ATTN
    pair = 2 * HEAD_PAD
    return pl.pallas_call(
        _attn_kernel,
        grid=(bsz, MLA_HEADS // 2),
        in_specs=[
            pl.BlockSpec((1, n, pair), lambda b, h: (b, 0, h)),
            pl.BlockSpec((1, n_ctx, pair), lambda b, h: (b, 0, h)),
            pl.BlockSpec((1, n, pair), lambda b, h: (b, 0, h)),
            pl.BlockSpec((1, n_ctx, pair), lambda b, h: (b, 0, h)),
            pl.BlockSpec((1, n, pair), lambda b, h: (b, 0, h)),
        ],
        out_specs=pl.BlockSpec((1, n, 2 * V_HEAD), lambda b, h: (b, 0, h)),
        out_shape=jax.ShapeDtypeStruct((bsz, n, MLA_HEADS * V_HEAD), BF16),
        scratch_shapes=[pltpu.VMEM((tq, n_ctx + n), F32), pltpu.VMEM((tq, n_ctx + n), F32),
                        pltpu.VMEM((tq, LANES), F32), pltpu.VMEM((tq, LANES), F32)],
        compiler_params=_params(("parallel", "parallel")),
        name="attention",
    )(q, k_ctx, k, v_ctx, v)


def _ffn_tail(x1, mod_ref, g2n_ref, w1_ref, w3_ref, w2_ref, out_ref):
    h2 = _norm_mod(x1, g2n_ref[...], mod_ref[0, 3:4, :], mod_ref[0, 4:5, :]).astype(BF16)
    acc = None
    for c0, cw in FF_CHUNKS:
        a = _dot(h2, w1_ref[:, c0:c0 + cw])
        b = _dot(h2, w3_ref[:, c0:c0 + cw])
        part = _dot((_silu(a) * b).astype(BF16), w2_ref[c0:c0 + cw, :])
        acc = part if acc is None else acc + part
    out_ref[0] = x1 + mod_ref[0, 5:6, :] * acc


def _even_out_kernel(x_ref, o_ref, y_ref, mod_ref, g2n_ref, wo_ref, w1_ref, w3_ref, w2_ref, out_ref):
    hv = MLA_HEADS * V_HEAD
    mix = _dot(o_ref[0], wo_ref[0:hv, :]) + _dot(y_ref[0].astype(BF16), wo_ref[hv:, :])
    x1 = x_ref[0] + mod_ref[0, 2:3, :] * mix
    _ffn_tail(x1, mod_ref, g2n_ref, w1_ref, w3_ref, w2_ref, out_ref)


def _even_out_call(x, o, y, mod, g2n, wo, w1, w3, w2):
    bsz, n, d = x.shape
    tm = TM_OUT
    tok = lambda width: pl.BlockSpec((1, tm, width), lambda b, i: (b, i, 0))
    return pl.pallas_call(
        _even_out_kernel,
        grid=(bsz, n // tm),
        in_specs=[
            tok(d), tok(o.shape[-1]), tok(y.shape[-1]),
            pl.BlockSpec((1, 8, d), lambda b, i: (b, 0, 0)),
            _const_spec((1, d)),
            _const_spec(wo.shape), _const_spec(w1.shape), _const_spec(w3.shape), _const_spec(w2.shape),
        ],
        out_specs=tok(d),
        out_shape=jax.ShapeDtypeStruct((bsz, n, d), F32),
        compiler_params=_params(("parallel", "parallel")),
        name="even_out_ffn",
    )(x, o, y, mod, g2n, wo, w1, w3, w2)


CONV_RB = 64


SUBLANES = 8


def _conf_conv_block(cbuf, dw_ref, bias, r0, ls):
    rows = CONV_RB + SUBLANES
    acc = jnp.broadcast_to(bias, (CONV_RB, LANES))
    for r in range(SUBLANES):
        part = None
        for o in range(r, CONF_WIDTH + 1, SUBLANES):
            if o == 0:
                continue
            term = dw_ref[o - 1:o, ls] * cbuf[r0 + o - r:r0 + o - r + rows, ls]
            part = term if part is None else part + term
        acc = acc + part[r:r + CONV_RB]
    return acc


def _zero_row(val_row, zero_ref):
    bits = lax.bitcast_convert_type(val_row, jnp.int32) & zero_ref[...]
    return lax.bitcast_convert_type(bits, F32)


def _odd_kernel(xp_ref, x_ref, xn_ref, xlag_ref, mod_ref, modlag_ref, zero_ref, g1n_ref, win_ref, dw_ref, dwb_ref,
                lng_ref, lnb_ref, scw_ref, wout_ref, g2n_ref, w1_ref, w3_ref, w2_ref, out_ref,
                hb_ref, cbuf, sbuf, bbuf, mbuf, acc_ref, *, tiles_per_seq):
    tm = x_ref.shape[1]
    t = pl.program_id(0)

    @pl.when(t == 0)
    def _():
        mbuf[...] = jnp.zeros(mbuf.shape, mbuf.dtype)

    mix = _dot(mbuf[...], wout_ref[...])
    x1 = xlag_ref[0] + modlag_ref[0, 2:3, :] * mix
    h2 = _norm_mod(x1, g2n_ref[...], modlag_ref[0, 3:4, :], modlag_ref[0, 4:5, :]).astype(BF16)

    i = lax.rem(jnp.minimum(t, pl.num_programs(0) - 2), tiles_per_seq)
    last = tiles_per_seq - 1
    g = g1n_ref[...]
    sh = mod_ref[0, 0:1, :]
    sc = mod_ref[0, 1:2, :]
    hb_ref[0:HALO, :] = _norm_mod(xp_ref[0], g, sh, sc).astype(BF16)
    hb_ref[HALO:HALO + tm, :] = _norm_mod(x_ref[0], g, sh, sc).astype(BF16)
    hb_ref[HALO + tm:, :] = _norm_mod(xn_ref[0], g, sh, sc).astype(BF16)
    u = _dot(hb_ref[...], win_ref[...])

    conf = u[:, 0:CONF_W] * jax.nn.sigmoid(u[:, CONF_W:2 * CONF_W])
    cx = u[:, 2 * CONF_W + SC_W:2 * CONF_W + 2 * SC_W] * u[:, 2 * CONF_W + 2 * SC_W:]
    keep_lo = i > 0
    keep_hi = i < last
    cbuf[0:HALO, :] = jnp.where(keep_lo, conf[0:HALO], 0.0)
    cbuf[HALO:HALO + tm, :] = conf[HALO:HALO + tm]
    cbuf[HALO + tm:, :] = jnp.where(keep_hi, conf[HALO + tm:], 0.0)
    sbuf[0:HALO, :] = jnp.where(keep_lo, cx[0:HALO], 0.0)
    sbuf[HALO:HALO + tm, :] = cx[HALO:HALO + tm]
    sbuf[HALO + tm:, :] = jnp.where(keep_hi, cx[HALO + tm:], 0.0)
    bbuf[...] = u[HALO:HALO + tm, 2 * CONF_W:2 * CONF_W + SC_W]

    assert HALO == (CONF_WIDTH - 1) // 2 + 1
    pad_s = (SC_WIDTH - 1) // 2

    def conv_block(r0, lg, order_after):
        ls = slice(lg * LANES, (lg + 1) * LANES)
        bias = dwb_ref[:, ls]
        if order_after is not None:
            bias = bias + order_after
        acc = _conf_conv_block(cbuf, dw_ref, bias, r0, ls)
        mu = jnp.mean(acc, axis=-1, keepdims=True)
        cen = acc - mu
        var = jnp.mean(cen * cen, axis=-1, keepdims=True)
        yn = cen * lax.rsqrt(var + EPS) * lng_ref[:, ls] + lnb_ref[:, ls]
        mbuf[r0:r0 + CONV_RB, ls] = _silu(yn).astype(BF16)

        acc = None
        for j in range(SC_WIDTH):
            o = HALO + r0 + j - pad_s
            term = scw_ref[j:j + 1, ls] * sbuf[o:o + CONV_RB, ls]
            acc = term if acc is None else acc + term
        mbuf[r0:r0 + CONV_RB, CONF_W + lg * LANES:CONF_W + (lg + 1) * LANES] = (
            bbuf[r0:r0 + CONV_RB, ls] * acc).astype(BF16)
        return yn[0:1, :]

    blocks = [(r0, lg) for r0 in range(0, tm, CONV_RB) for lg in range(CONF_W // LANES)]
    n_groups = len(FF_CHUNKS) - 1
    after_conv = None
    for k, (c0, cw) in enumerate(FF_CHUNKS):
        a = _dot(h2, w1_ref[:, c0:c0 + cw])
        b = _dot(h2, w3_ref[:, c0:c0 + cw])
        part = _dot((_silu(a) * b).astype(BF16), w2_ref[c0:c0 + cw, :])
        if k == 0:
            acc_ref[...] = part
        else:
            acc_ref[...] += part
        if after_conv is not None:
            acc_ref[0:1, 0:LANES] += after_conv
            after_conv = None
        if k < n_groups:
            after_chunk = _zero_row(part[0:1, 0:LANES], zero_ref)
            tail = None
            for r0, lg in blocks[k * len(blocks) // n_groups:(k + 1) * len(blocks) // n_groups]:
                tail = conv_block(r0, lg, after_chunk)
            after_conv = _zero_row(tail, zero_ref)
    out_ref[0] = x1 + modlag_ref[0, 5:6, :] * acc_ref[...]


def _odd_call(x, mod, g1n, win, dw, dwb, lng, lnb, scw, wout, g2n, w1, w3, w2):
    bsz, n, d = x.shape
    tm = TM_ODD
    hb = tm // HALO
    nh = n // HALO
    nt = n // tm
    total = bsz * nt

    def cur(t):
        u = jnp.minimum(t, total - 1)
        return u // nt, u % nt

    def lag(t):
        u = jnp.maximum(t - 1, 0)
        return u // nt, u % nt

    def halo_lo(t):
        b, i = cur(t)
        return b, jnp.maximum(i * hb - 1, 0), 0

    def halo_hi(t):
        b, i = cur(t)
        return b, jnp.minimum((i + 1) * hb, nh - 1), 0

    return pl.pallas_call(
        functools.partial(_odd_kernel, tiles_per_seq=nt),
        grid=(total + 1,),
        in_specs=[
            pl.BlockSpec((1, HALO, d), halo_lo),
            pl.BlockSpec((1, tm, d), lambda t: (*cur(t), 0)),
            pl.BlockSpec((1, HALO, d), halo_hi),
            pl.BlockSpec((1, tm, d), lambda t: (*lag(t), 0)),
            pl.BlockSpec((1, 8, d), lambda t: (cur(t)[0], 0, 0)),
            pl.BlockSpec((1, 8, d), lambda t: (lag(t)[0], 0, 0)),
            _const_spec((1, LANES)),
            _const_spec((1, d)),
            _const_spec(win.shape),
            _const_spec(dw.shape), _const_spec(dwb.shape), _const_spec(lng.shape), _const_spec(lnb.shape),
            _const_spec(scw.shape),
            _const_spec(wout.shape),
            _const_spec((1, d)),
            _const_spec(w1.shape), _const_spec(w3.shape), _const_spec(w2.shape),
        ],
        out_specs=pl.BlockSpec((1, tm, d), lambda t: (*lag(t), 0)),
        out_shape=jax.ShapeDtypeStruct((bsz, n, d), F32),
        scratch_shapes=[
            pltpu.VMEM((tm + 2 * HALO, d), BF16),
            pltpu.VMEM((tm + 2 * HALO, CONF_W), F32),
            pltpu.VMEM((tm + 2 * HALO, SC_W), F32),
            pltpu.VMEM((tm, SC_W), F32),
            pltpu.VMEM((tm, CONF_W + SC_W), BF16),
            pltpu.VMEM((tm, d), F32),
        ],
        compiler_params=_params(("arbitrary",)),
        name="odd_layer",
    )(x, x, x, x, mod, mod, jnp.zeros((1, LANES), jnp.int32), g1n, win, dw, dwb, lng, lnb, scw, wout, g2n,
      w1, w3, w2)


def _dft_tables():
    r = FFT_R
    gw = FNET_GROUP_W
    idx = np.arange(gw)
    ang = 2.0 * np.pi * ((idx[:, None] * idx[None, :]) % gw) / gw
    cs = np.concatenate([np.cos(ang), -np.sin(ang)], axis=1) / np.sqrt(gw)
    a = np.arange(r)
    nb = DFT_NB
    eye = np.eye(nb)
    ang1 = 2.0 * np.pi * ((a[:, None] * a[None, :]) % r) / r
    fr, fi = np.cos(ang1) / 8.0, -np.sin(ang1) / 8.0
    a_re = np.concatenate([np.kron(fr, eye), np.kron(fi, eye)], axis=0)
    a_im = np.concatenate([np.kron(-fi, eye), np.kron(fr, eye)], axis=0)
    n = r * r
    c = a[:, None, None]
    dd = a[None, :, None]
    bb = a[None, None, :]
    ang2 = 2.0 * np.pi * ((bb * (c + r * dd)) % n) / n
    gr, gi_neg = np.cos(ang2) / 8.0, np.sin(ang2) / 8.0
    b_re = np.zeros((r // nb, r, nb, nb, r))
    b_im = np.zeros((r // nb, r, nb, nb, r))
    for ci in range(nb):
        b_re[:, :, ci, ci, :] = gr[ci::nb]
        b_im[:, :, ci, ci, :] = gi_neg[ci::nb]
    b_re = b_re.reshape(r // nb, r * nb, nb * r)
    b_im = b_im.reshape(r // nb, r * nb, nb * r)
    named = (("cs", cs), ("a_re", a_re), ("a_im", a_im), ("b_re", b_re), ("b_im", b_im))
    return {name: jnp.asarray(t, F32).astype(BF16) for name, t in named}


def _rope_tables(n):
    rows = n // GRID_W
    row = jnp.broadcast_to(jnp.arange(rows, dtype=F32)[:, None], (rows, GRID_W)).reshape(n)
    col = jnp.broadcast_to(jnp.arange(GRID_W, dtype=F32)[None, :], (rows, GRID_W)).reshape(n)
    per_axis = QK_ROPE // 4
    inv_freq = ROPE_BASE ** (-jnp.arange(per_axis, dtype=F32) / per_axis)
    ang = jnp.concatenate([row[:, None] * inv_freq, col[:, None] * inv_freq], axis=-1)
    cos, sin = jnp.cos(ang), jnp.sin(ang)
    tail = LANES - QK_HEAD
    rc = jnp.concatenate([jnp.ones((n, QK_NOPE), F32), cos, cos, jnp.ones((n, tail), F32)], axis=-1)
    rs = jnp.concatenate([jnp.zeros((n, QK_NOPE), F32), -sin, sin, jnp.zeros((n, tail), F32)], axis=-1)
    return rc, rs


def _swap_rope_halves(t):
    r1 = t[..., QK_NOPE:QK_NOPE + ROPE_HALF]
    r2 = t[..., QK_NOPE + ROPE_HALF:]
    return jnp.concatenate([jnp.zeros_like(t[..., :QK_NOPE]), r2, r1], axis=-1)


def _pad_heads(w, width):
    r = w.shape[0]
    return jnp.pad(w, ((0, 0), (0, 0), (0, HEAD_PAD - width))).reshape(r, MLA_HEADS * HEAD_PAD)


def _pad_gain(g):
    return jnp.pad(g, (0, HEAD_PAD - QK_HEAD)).reshape(1, HEAD_PAD)


def _pad_values(w):
    r = w.shape[0]
    pairs = w.reshape(r, MLA_HEADS // 2, 2, V_HEAD)
    zero = jnp.zeros_like(pairs[:, :, 0])
    even = jnp.concatenate([pairs[:, :, 0], zero], axis=-1)
    odd = jnp.concatenate([zero, pairs[:, :, 1]], axis=-1)
    return jnp.stack([even, odd], axis=2).reshape(r, MLA_HEADS * HEAD_PAD)


def _value_ones_row():
    row = np.zeros((MLA_HEADS, HEAD_PAD), np.float32)
    row[0::2, V_HEAD] = 1.0
    row[1::2, 0] = 1.0
    return jnp.asarray(row.reshape(1, MLA_HEADS * HEAD_PAD))


def kernel(x, c, ctx, c_ctx, ada_w, ada_b, norm1_g, norm2_g, ffn_w1, ffn_w3, ffn_w2, a_w_in, a_q_ln_g, a_kv_ln_g, a_w_uq, a_w_uk, a_w_uv, a_q_norm_g, a_k_norm_g, a_w_out, b_w_in, b_conf_dw, b_conf_dw_b, b_conf_ln_g, b_conf_ln_b, b_sc_dw, b_w_out):
    bsz, n, d = x.shape
    depth = ada_w.shape[0]
    assert depth == 2 and d == D_MODEL and n == FFT_R * FFT_R and ctx.shape[1] == CTX_LEN

    rows = 16
    cvec = jnp.concatenate([c, c_ctx[None, :], jnp.zeros((rows - bsz - 1, d), F32)], axis=0)
    ada = _ada_call(cvec, ada_w, ada_b).reshape(depth, rows, 6, d)
    unit = jnp.array([0.0, 1.0, 0.0, 0.0, 1.0, 0.0], F32)[None, None, :, None]
    mods = jnp.pad(ada + unit, ((0, 0), (0, 0), (0, 2), (0, 0)))

    tables = _dft_tables()
    tables["rc"], tables["rs"] = _rope_tables(n)
    tables["ones"] = jnp.ones((LANES, LANES), BF16)

    w_in = a_w_in[0]
    kr_cols = jnp.pad(w_in[:, Q_LORA + KV_LORA:EVEN_KV_END], ((0, 0), (QK_NOPE, LANES - QK_HEAD)))
    w = {
        "wq": jnp.concatenate([w_in[:, :Q_LORA], kr_cols], axis=1).astype(BF16),
        "wkv": w_in[:, Q_LORA:Q_LORA + KV_LORA].astype(BF16),
        "wf": w_in[:, EVEN_KV_END:].astype(BF16),
        "qlg": a_q_ln_g[0].reshape(1, Q_LORA),
        "kvlg": a_kv_ln_g[0].reshape(1, KV_LORA),
        "wuq": _pad_heads(a_w_uq[0], QK_HEAD).astype(BF16),
        "wuqs": _pad_heads(_swap_rope_halves(a_w_uq[0]), QK_HEAD).astype(BF16),
        "wuk": _pad_heads(a_w_uk[0], QK_NOPE).astype(BF16),
        "wuv": _pad_values(a_w_uv[0]).astype(BF16),
        "vone": _value_ones_row(),
        "gq": _pad_gain(a_q_norm_g[0]),
        "gqs": _pad_gain(_swap_rope_halves(a_q_norm_g[0])),
        "gk": _pad_gain(a_k_norm_g[0]),
    }
    g1 = norm1_g[0].reshape(1, d)
    q, k, v, z = _even_in_call(x, mods[0, :bsz], g1, w, tables)
    k_ctx, v_ctx = _ctx_in_call(ctx, mods[0, bsz:bsz + 1], g1, w, tables)
    y = _seq_dft_call(z, tables)
    o = _attn_call(q, k_ctx, k, v_ctx, v)
    x = _even_out_call(x, o, y, mods[0, :bsz], norm2_g[0].reshape(1, d), a_w_out[0].astype(BF16),
                       ffn_w1[0].astype(BF16), ffn_w3[0].astype(BF16), ffn_w2[0].astype(BF16))

    x = _odd_call(x, mods[1, :bsz], norm1_g[1].reshape(1, d), b_w_in[0].astype(BF16),
                  b_conf_dw[0], b_conf_dw_b[0].reshape(1, CONF_W), b_conf_ln_g[0].reshape(1, CONF_W),
                  b_conf_ln_b[0].reshape(1, CONF_W), b_sc_dw[0], b_w_out[0].astype(BF16),
                  norm2_g[1].reshape(1, d), ffn_w1[1].astype(BF16), ffn_w3[1].astype(BF16),
                  ffn_w2[1].astype(BF16))
    return x
```

```python
import functools
import math

import numpy as np
import jax
import jax.numpy as jnp
from jax import lax
from jax.experimental import pallas as pl
from jax.experimental.pallas import tpu as pltpu

F32 = jnp.float32
BF16 = jnp.bfloat16

D_MODEL = 1024
CTX_LEN = 256
GRID_W = 64
EPS = 1e-6
MLA_HEADS = 8
QK_NOPE = 64
QK_ROPE = 32
QK_HEAD = QK_NOPE + QK_ROPE
V_HEAD = 64
Q_LORA = 384
KV_LORA = 256
ROPE_BASE = 10000.0
FNET_GROUPS = 4
FNET_GROUP_W = 128
FNET_W = FNET_GROUPS * FNET_GROUP_W
CONF_GROUPS = 4
CONF_W = 512
CONF_WIDTH = 31
SC_W = 512
SC_WIDTH = 3
D_FF = 2816
EVEN_KV_END = Q_LORA + KV_LORA + QK_ROPE

LANES = 128
HEAD_PAD = LANES
ROPE_HALF = QK_ROPE // 2
FFT_R = 64
HALO = 16
VMEM_LIMIT = 56 * 1024 * 1024

TM_EVEN_IN = 1024
TQ_ATTN = 512
KC_ATTN = 512
TM_OUT = 512
TM_ODD = 512
FF_CHUNKS = tuple((c, min(512, D_FF - c)) for c in range(0, D_FF, 512))

Q_SCALE = QK_HEAD ** -0.5 * math.log2(math.e)


def _dot(a, b):
    return jnp.dot(a, b, preferred_element_type=F32)


def _dot_nt(a, b):
    return lax.dot_general(a, b, (((1,), (1,)), ((), ())), preferred_element_type=F32)


def _inv_rms(x, n):
    return lax.rsqrt(jnp.sum(x * x, axis=-1, keepdims=True) * (1.0 / n) + EPS)


def _norm_mod(x, g, shift, scale1p):
    return (x * _inv_rms(x, x.shape[-1]) * g) * scale1p + shift


def _silu(a):
    return a * jax.nn.sigmoid(a)


def _const_spec(shape):
    nd = len(shape)
    return pl.BlockSpec(shape, lambda *_: (0,) * nd, pipeline_mode=pl.Buffered(1))


def _params(sem, flags=None):
    return pltpu.CompilerParams(dimension_semantics=sem, vmem_limit_bytes=VMEM_LIMIT, flags=flags)


def _ada_kernel(c_ref, w_ref, b_ref, o_ref):
    s = _silu(c_ref[...]).astype(BF16)
    o_ref[0] = _dot(s, w_ref[0].astype(BF16)) + b_ref[0]


def _ada_call(cvec, ada_w, ada_b):
    depth, d, n6 = ada_w.shape
    rows = cvec.shape[0]
    tn = 1536
    return pl.pallas_call(
        _ada_kernel,
        grid=(depth, n6 // tn),
        in_specs=[
            pl.BlockSpec((rows, d), lambda l, j: (0, 0)),
            pl.BlockSpec((1, d, tn), lambda l, j: (l, 0, j)),
            pl.BlockSpec((1, 1, tn), lambda l, j: (l, 0, j)),
        ],
        out_specs=pl.BlockSpec((1, rows, tn), lambda l, j: (l, 0, j)),
        out_shape=jax.ShapeDtypeStruct((depth, rows, n6), F32),
        compiler_params=_params(("arbitrary", "arbitrary")),
        name="ada_mod",
    )(cvec, ada_w, ada_b.reshape(depth, 1, n6))


def _head_inv_rms(t, ones):
    ss = _dot((t * t).astype(BF16), ones)
    return lax.rsqrt(ss * (1.0 / QK_HEAD) + EPS)


def _keys_values(hb, krb, rope, wkv_ref, kvlg_ref, wuk_ref, wuv_ref, vone_ref, gk_ref, ones_ref, k_ref, v_ref):
    ckv = _dot(hb, wkv_ref[...])
    ckvn = (ckv * _inv_rms(ckv, KV_LORA) * kvlg_ref[...]).astype(BF16)
    kf = _dot(ckvn, wuk_ref[...])
    v_ref[0] = (_dot(ckvn, wuv_ref[...]) + vone_ref[...]).astype(BF16)
    gk = gk_ref[...]
    krg = krb * gk
    if rope is not None:
        rc, rs = rope
        lane = lax.broadcasted_iota(jnp.int32, krg.shape, 1)
        partner = jnp.where(lane < QK_NOPE + ROPE_HALF, pltpu.roll(krg, LANES - ROPE_HALF, 1),
                            pltpu.roll(krg, ROPE_HALF, 1))
        krg = krg * rc + partner * rs
    ones = ones_ref[...]
    for h in range(MLA_HEADS):
        kn = kf[:, h * HEAD_PAD:(h + 1) * HEAD_PAD]
        r = _head_inv_rms(kn + krb, ones)
        k_ref[0, :, h * HEAD_PAD:(h + 1) * HEAD_PAD] = ((kn * gk + krg) * r).astype(BF16)


def _even_in_kernel(x_ref, mod_ref, g_ref, wq_ref, wkv_ref, wf_ref, qlg_ref, kvlg_ref, wuq_ref, wuqs_ref, wuk_ref,
                    wuv_ref, vone_ref, gq_ref, gqs_ref, gk_ref, rc_ref, rs_ref, cs_ref, ones_ref,
                    q_ref, k_ref, v_ref, z_ref):
    hb = _norm_mod(x_ref[0], g_ref[...], mod_ref[0, 0:1, :], mod_ref[0, 1:2, :]).astype(BF16)
    rc, rs = rc_ref[...], rs_ref[...]
    uq = _dot(hb, wq_ref[...])
    _keys_values(hb, uq[:, Q_LORA:], (rc, rs), wkv_ref, kvlg_ref, wuk_ref, wuv_ref, vone_ref, gk_ref, ones_ref,
                 k_ref, v_ref)

    cq = uq[:, :Q_LORA]
    cqn = (cq * _inv_rms(cq, Q_LORA) * qlg_ref[...]).astype(BF16)
    qf = _dot(cqn, wuq_ref[...])
    qs = _dot(cqn, wuqs_ref[...])
    gq = gq_ref[...] * Q_SCALE
    gqs = gqs_ref[...] * Q_SCALE
    ones = ones_ref[...]
    for h in range(MLA_HEADS):
        sl = slice(h * HEAD_PAD, (h + 1) * HEAD_PAD)
        qh = qf[:, sl]
        r = _head_inv_rms(qh, ones)
        q_ref[0, :, sl] = ((qh * gq * rc + qs[:, sl] * gqs * rs) * r).astype(BF16)

    uf = _dot(hb, wf_ref[...]).astype(BF16)
    for g in range(FNET_GROUPS):
        zz = _dot(uf[:, g * LANES:(g + 1) * LANES], cs_ref[...])
        z_ref[0, :, g * LANES:(g + 1) * LANES] = zz[:, :LANES]
        z_ref[0, :, FNET_W + g * LANES:FNET_W + (g + 1) * LANES] = zz[:, LANES:]


def _ctx_in_kernel(x_ref, mod_ref, g_ref, wq_ref, wkv_ref, kvlg_ref, wuk_ref, wuv_ref, vone_ref, gk_ref, ones_ref,
                   k_ref, v_ref):
    hb = _norm_mod(x_ref[0], g_ref[...], mod_ref[0, 0:1, :], mod_ref[0, 1:2, :]).astype(BF16)
    krb = _dot(hb, wq_ref[:, Q_LORA:])
    _keys_values(hb, krb, None, wkv_ref, kvlg_ref, wuk_ref, wuv_ref, vone_ref, gk_ref, ones_ref, k_ref, v_ref)


def _even_in_call(x, mod, g, w, tables):
    bsz, n, d = x.shape
    tm = TM_EVEN_IN
    hw = MLA_HEADS * HEAD_PAD
    tok = lambda width: pl.BlockSpec((1, tm, width), lambda b, i: (b, i, 0))
    tab = pl.BlockSpec((tm, LANES), lambda b, i: (i, 0))
    return pl.pallas_call(
        _even_in_kernel,
        grid=(bsz, n // tm),
        in_specs=[
            tok(d),
            pl.BlockSpec((1, 8, d), lambda b, i: (b, 0, 0)),
            _const_spec((1, d)),
            _const_spec(w["wq"].shape), _const_spec(w["wkv"].shape), _const_spec(w["wf"].shape),
            _const_spec((1, Q_LORA)), _const_spec((1, KV_LORA)),
            _const_spec(w["wuq"].shape), _const_spec(w["wuqs"].shape), _const_spec(w["wuk"].shape),
            _const_spec(w["wuv"].shape), _const_spec((1, hw)),
            _const_spec((1, LANES)), _const_spec((1, LANES)), _const_spec((1, LANES)),
            tab, tab,
            _const_spec(tables["cs"].shape), _const_spec(tables["ones"].shape),
        ],
        out_specs=[tok(hw), tok(hw), tok(hw), tok(2 * FNET_W)],
        out_shape=[
            jax.ShapeDtypeStruct((bsz, n, hw), BF16),
            jax.ShapeDtypeStruct((bsz, n, hw), BF16),
            jax.ShapeDtypeStruct((bsz, n, hw), BF16),
            jax.ShapeDtypeStruct((bsz, n, 2 * FNET_W), F32),
        ],
        compiler_params=_params(("parallel", "parallel")),
        name="even_in",
    )(x, mod, g, w["wq"], w["wkv"], w["wf"], w["qlg"], w["kvlg"], w["wuq"], w["wuqs"], w["wuk"], w["wuv"],
      w["vone"], w["gq"], w["gqs"], w["gk"], tables["rc"], tables["rs"], tables["cs"], tables["ones"])


def _ctx_in_call(ctx, mod, g, w, tables):
    bsz, n, d = ctx.shape
    hw = MLA_HEADS * HEAD_PAD
    tok = lambda width: pl.BlockSpec((1, n, width), lambda b: (b, 0, 0))
    return pl.pallas_call(
        _ctx_in_kernel,
        grid=(bsz,),
        in_specs=[
            tok(d),
            pl.BlockSpec((1, 8, d), lambda b: (0, 0, 0)),
            _const_spec((1, d)),
            _const_spec(w["wq"].shape), _const_spec(w["wkv"].shape),
            _const_spec((1, KV_LORA)),
            _const_spec(w["wuk"].shape), _const_spec(w["wuv"].shape), _const_spec((1, hw)),
            _const_spec((1, LANES)), _const_spec(tables["ones"].shape),
        ],
        out_specs=[tok(hw), tok(hw)],
        out_shape=[
            jax.ShapeDtypeStruct((bsz, n, hw), BF16),
            jax.ShapeDtypeStruct((bsz, n, hw), BF16),
        ],
        compiler_params=_params(("parallel",)),
        name="ctx_in",
    )(ctx, mod, g, w["wq"], w["wkv"], w["kvlg"], w["wuk"], w["wuv"], w["vone"], w["gk"], tables["ones"])


DFT_NB = 8
DFT_GROUPS = 2


def _seq_dft_kernel(z_ref, are_ref, aim_ref, bre_ref, bim_ref, y_ref, t_scr):
    r = FFT_R
    half = r // (DFT_NB * DFT_GROUPS)
    s = pl.program_id(1)

    @pl.when(s < half)
    def _():
        for g in range(DFT_GROUPS):
            lo = g * DFT_NB
            z = z_ref[0, :, lo:lo + DFT_NB, :].reshape(r * DFT_NB, 2 * FNET_W).astype(BF16)
            t = _dot(are_ref[...], z[:, :FNET_W]) + _dot(aim_ref[...], z[:, FNET_W:])
            b0 = pl.multiple_of(s * (DFT_NB * DFT_GROUPS) + lo, DFT_NB)
            t_scr[:, pl.ds(b0, DFT_NB), :] = t.reshape(2 * r, DFT_NB, FNET_W)

    @pl.when(s >= half)
    def _():
        for g in range(DFT_GROUPS):
            lo = g * DFT_NB
            j = (s - half) * DFT_GROUPS + g
            c0 = pl.multiple_of(j * DFT_NB, DFT_NB)
            xr = t_scr[pl.ds(c0, DFT_NB)].reshape(DFT_NB * r, FNET_W).astype(BF16)
            xi = t_scr[pl.ds(r + c0, DFT_NB)].reshape(DFT_NB * r, FNET_W).astype(BF16)
            y = _dot(bre_ref[j], xr) + _dot(bim_ref[j], xi)
            y_ref[0, :, lo:lo + DFT_NB, :] = y.reshape(r, DFT_NB, FNET_W)


def _seq_dft_call(z, tables):
    bsz, n, w2 = z.shape
    r = FFT_R
    nb = DFT_NB * DFT_GROUPS
    half = r // nb
    y = pl.pallas_call(
        _seq_dft_kernel,
        grid=(bsz, 2 * half),
        in_specs=[
            pl.BlockSpec((1, r, nb, w2), lambda b, s: (b, 0, jnp.minimum(s, half - 1), 0)),
            _const_spec(tables["a_re"].shape), _const_spec(tables["a_im"].shape),
            _const_spec(tables["b_re"].shape), _const_spec(tables["b_im"].shape),
        ],
        out_specs=pl.BlockSpec((1, r, nb, FNET_W), lambda b, s: (b, 0, jnp.maximum(s - half, 0), 0)),
        out_shape=jax.ShapeDtypeStruct((bsz, r, r, FNET_W), F32),
        scratch_shapes=[pltpu.VMEM((2 * r, r, FNET_W), F32)],
        compiler_params=_params(("parallel", "arbitrary")),
        name="seq_dft",
    )(z.reshape(bsz, r, r, w2), tables["a_re"], tables["a_im"], tables["b_re"], tables["b_im"])
    return y.reshape(bsz, n, FNET_W)


def _attn_kernel(q_ref, kc_ref, k_ref, vc_ref, v_ref, o_ref, s0_ref, s1_ref, m0_ref, m1_ref):
    n_ctx = kc_ref.shape[1]
    n_chunks = k_ref.shape[1] // KC_ATTN
    tq = TQ_ATTN
    n_tiles = q_ref.shape[1] // tq
    s_refs = (s0_ref, s1_ref)
    m_refs = (m0_ref, m1_ref)
    chunks = [(kc_ref, vc_ref, 0, n_ctx, 0)] + [
        (k_ref, v_ref, j * KC_ATTN, KC_ATTN, n_ctx + j * KC_ATTN) for j in range(n_chunks)]

    def lane_groups(x):
        return [x[:, t * LANES:(t + 1) * LANES] for t in range(x.shape[1] // LANES)]

    def segment(i, h_scores, h_values):
        lo = h_scores * HEAD_PAD
        q = q_ref[0, pl.ds(pl.multiple_of(i * tq, tq), tq), lo:lo + HEAD_PAD]
        s_out = s_refs[h_scores]
        m_run = acc = None
        if h_values is not None:
            s_in = s_refs[h_values]
            m_in = m_refs[h_values][...]
            lv = h_values * HEAD_PAD
        for kr, vr, r0, width, c0 in chunks:
            if h_values is not None:
                p = [jnp.exp2(g - m_in) for g in lane_groups(s_in[:, c0:c0 + width])]
                pv = _dot(jnp.concatenate(p, axis=1).astype(BF16), vr[0, r0:r0 + width, lv:lv + HEAD_PAD])
                acc = pv if acc is None else acc + pv
            s = _dot_nt(q, kr[0, r0:r0 + width, lo:lo + HEAD_PAD])
            s_out[:, c0:c0 + width] = s
            m_run = functools.reduce(jnp.maximum, lane_groups(s) if m_run is None else [m_run] + lane_groups(s))
        m_refs[h_scores][...] = jnp.broadcast_to(jnp.max(m_run, axis=-1, keepdims=True), (tq, LANES))
        return acc

    segment(0, 0, None)

    def body(i, carry):
        acc0 = segment(i, 1, 0)
        acc1 = segment(jnp.minimum(i + 1, n_tiles - 1), 0, 1)
        out0 = acc0 / acc0[:, V_HEAD:V_HEAD + 1]
        out1 = acc1 / acc1[:, 0:1]
        lane = lax.broadcasted_iota(jnp.int32, out0.shape, 1)
        o_ref[0, pl.ds(pl.multiple_of(i * tq, tq), tq), :] = jnp.where(lane < V_HEAD, out0, out1).astype(BF16)
        return carry

    lax.fori_loop(0, n_tiles, body, 0, unroll=4)


def _attn_call(q, k_ctx, k, v_ctx, v):
    bsz, n, hw = q.shape
    n_ctx = k_ctx.shape[1]
    tq = TQ_ATTN
    pair = 2 * HEAD_PAD
    return pl.pallas_call(
        _attn_kernel,
        grid=(bsz, MLA_HEADS // 2),
        in_specs=[
            pl.BlockSpec((1, n, pair), lambda b, h: (b, 0, h)),
            pl.BlockSpec((1, n_ctx, pair), lambda b, h: (b, 0, h)),
            pl.BlockSpec((1, n, pair), lambda b, h: (b, 0, h)),
            pl.BlockSpec((1, n_ctx, pair), lambda b, h: (b, 0, h)),
            pl.BlockSpec((1, n, pair), lambda b, h: (b, 0, h)),
        ],
        out_specs=pl.BlockSpec((1, n, 2 * V_HEAD), lambda b, h: (b, 0, h)),
        out_shape=jax.ShapeDtypeStruct((bsz, n, MLA_HEADS * V_HEAD), BF16),
        scratch_shapes=[pltpu.VMEM((tq, n_ctx + n), F32), pltpu.VMEM((tq, n_ctx + n), F32),
                        pltpu.VMEM((tq, LANES), F32), pltpu.VMEM((tq, LANES), F32)],
        compiler_params=_params(("parallel", "parallel")),
        name="attention",
    )(q, k_ctx, k, v_ctx, v)


def _ffn_tail(x1, mod_ref, g2n_ref, w1_ref, w3_ref, w2_ref, out_ref):
    h2 = _norm_mod(x1, g2n_ref[...], mod_ref[0, 3:4, :], mod_ref[0, 4:5, :]).astype(BF16)
    acc = None
    for c0, cw in FF_CHUNKS:
        a = _dot(h2, w1_ref[:, c0:c0 + cw])
        b = _dot(h2, w3_ref[:, c0:c0 + cw])
        part = _dot((_silu(a) * b).astype(BF16), w2_ref[c0:c0 + cw, :])
        acc = part if acc is None else acc + part
    out_ref[0] = x1 + mod_ref[0, 5:6, :] * acc


def _even_out_kernel(x_ref, o_ref, y_ref, mod_ref, g2n_ref, wo_ref, w1_ref, w3_ref, w2_ref, out_ref):
    hv = MLA_HEADS * V_HEAD
    mix = _dot(o_ref[0], wo_ref[0:hv, :]) + _dot(y_ref[0].astype(BF16), wo_ref[hv:, :])
    x1 = x_ref[0] + mod_ref[0, 2:3, :] * mix
    _ffn_tail(x1, mod_ref, g2n_ref, w1_ref, w3_ref, w2_ref, out_ref)


def _even_out_call(x, o, y, mod, g2n, wo, w1, w3, w2):
    bsz, n, d = x.shape
    tm = TM_OUT
    tok = lambda width: pl.BlockSpec((1, tm, width), lambda b, i: (b, i, 0))
    return pl.pallas_call(
        _even_out_kernel,
        grid=(bsz, n // tm),
        in_specs=[
            tok(d), tok(o.shape[-1]), tok(y.shape[-1]),
            pl.BlockSpec((1, 8, d), lambda b, i: (b, 0, 0)),
            _const_spec((1, d)),
            _const_spec(wo.shape), _const_spec(w1.shape), _const_spec(w3.shape), _const_spec(w2.shape),
        ],
        out_specs=tok(d),
        out_shape=jax.ShapeDtypeStruct((bsz, n, d), F32),
        compiler_params=_params(("parallel", "parallel")),
        name="even_out_ffn",
    )(x, o, y, mod, g2n, wo, w1, w3, w2)


CONV_RB = 64


SUBLANES = 8


def _conf_conv_block(cbuf, dw_ref, bias, r0, ls):
    rows = CONV_RB + SUBLANES
    acc = jnp.broadcast_to(bias, (CONV_RB, LANES))
    for r in range(SUBLANES):
        part = None
        for o in range(r, CONF_WIDTH + 1, SUBLANES):
            if o == 0:
                continue
            term = dw_ref[o - 1:o, ls] * cbuf[r0 + o - r:r0 + o - r + rows, ls]
            part = term if part is None else part + term
        acc = acc + part[r:r + CONV_RB]
    return acc


def _zero_row(val_row, zero_ref):
    bits = lax.bitcast_convert_type(val_row, jnp.int32) & zero_ref[...]
    return lax.bitcast_convert_type(bits, F32)


def _odd_kernel(xp_ref, x_ref, xn_ref, xlag_ref, mod_ref, modlag_ref, zero_ref, g1n_ref, win_ref, dw_ref, dwb_ref,
                lng_ref, lnb_ref, scw_ref, wout_ref, g2n_ref, w1_ref, w3_ref, w2_ref, out_ref,
                hb_ref, cbuf, sbuf, bbuf, mbuf, acc_ref, *, tiles_per_seq):
    tm = x_ref.shape[1]
    t = pl.program_id(0)

    @pl.when(t == 0)
    def _():
        mbuf[...] = jnp.zeros(mbuf.shape, mbuf.dtype)

    mix = _dot(mbuf[...], wout_ref[...])
    x1 = xlag_ref[0] + modlag_ref[0, 2:3, :] * mix
    h2 = _norm_mod(x1, g2n_ref[...], modlag_ref[0, 3:4, :], modlag_ref[0, 4:5, :]).astype(BF16)

    i = lax.rem(jnp.minimum(t, pl.num_programs(0) - 2), tiles_per_seq)
    last = tiles_per_seq - 1
    g = g1n_ref[...]
    sh = mod_ref[0, 0:1, :]
    sc = mod_ref[0, 1:2, :]
    hb_ref[0:HALO, :] = _norm_mod(xp_ref[0], g, sh, sc).astype(BF16)
    hb_ref[HALO:HALO + tm, :] = _norm_mod(x_ref[0], g, sh, sc).astype(BF16)
    hb_ref[HALO + tm:, :] = _norm_mod(xn_ref[0], g, sh, sc).astype(BF16)
    u = _dot(hb_ref[...], win_ref[...])

    conf = u[:, 0:CONF_W] * jax.nn.sigmoid(u[:, CONF_W:2 * CONF_W])
    cx = u[:, 2 * CONF_W + SC_W:2 * CONF_W + 2 * SC_W] * u[:, 2 * CONF_W + 2 * SC_W:]
    keep_lo = i > 0
    keep_hi = i < last
    cbuf[0:HALO, :] = jnp.where(keep_lo, conf[0:HALO], 0.0)
    cbuf[HALO:HALO + tm, :] = conf[HALO:HALO + tm]
    cbuf[HALO + tm:, :] = jnp.where(keep_hi, conf[HALO + tm:], 0.0)
    sbuf[0:HALO, :] = jnp.where(keep_lo, cx[0:HALO], 0.0)
    sbuf[HALO:HALO + tm, :] = cx[HALO:HALO + tm]
    sbuf[HALO + tm:, :] = jnp.where(keep_hi, cx[HALO + tm:], 0.0)
    bbuf[...] = u[HALO:HALO + tm, 2 * CONF_W:2 * CONF_W + SC_W]

    assert HALO == (CONF_WIDTH - 1) // 2 + 1
    pad_s = (SC_WIDTH - 1) // 2

    def conv_block(r0, lg, order_after):
        ls = slice(lg * LANES, (lg + 1) * LANES)
        bias = dwb_ref[:, ls]
        if order_after is not None:
            bias = bias + order_after
        acc = _conf_conv_block(cbuf, dw_ref, bias, r0, ls)
        mu = jnp.mean(acc, axis=-1, keepdims=True)
        cen = acc - mu
        var = jnp.mean(cen * cen, axis=-1, keepdims=True)
        yn = cen * lax.rsqrt(var + EPS) * lng_ref[:, ls] + lnb_ref[:, ls]
        mbuf[r0:r0 + CONV_RB, ls] = _silu(yn).astype(BF16)

        acc = None
        for j in range(SC_WIDTH):
            o = HALO + r0 + j - pad_s
            term = scw_ref[j:j + 1, ls] * sbuf[o:o + CONV_RB, ls]
            acc = term if acc is None else acc + term
        mbuf[r0:r0 + CONV_RB, CONF_W + lg * LANES:CONF_W + (lg + 1) * LANES] = (
            bbuf[r0:r0 + CONV_RB, ls] * acc).astype(BF16)
        return yn[0:1, :]

    blocks = [(r0, lg) for r0 in range(0, tm, CONV_RB) for lg in range(CONF_W // LANES)]
    n_groups = len(FF_CHUNKS) - 1
    after_conv = None
    for k, (c0, cw) in enumerate(FF_CHUNKS):
        a = _dot(h2, w1_ref[:, c0:c0 + cw])
        b = _dot(h2, w3_ref[:, c0:c0 + cw])
        part = _dot((_silu(a) * b).astype(BF16), w2_ref[c0:c0 + cw, :])
        if k == 0:
            acc_ref[...] = part
        else:
            acc_ref[...] += part
        if after_conv is not None:
            acc_ref[0:1, 0:LANES] += after_conv
            after_conv = None
        if k < n_groups:
            after_chunk = _zero_row(part[0:1, 0:LANES], zero_ref)
            tail = None
            for r0, lg in blocks[k * len(blocks) // n_groups:(k + 1) * len(blocks) // n_groups]:
                tail = conv_block(r0, lg, after_chunk)
            after_conv = _zero_row(tail, zero_ref)
    out_ref[0] = x1 + modlag_ref[0, 5:6, :] * acc_ref[...]


def _odd_call(x, mod, g1n, win, dw, dwb, lng, lnb, scw, wout, g2n, w1, w3, w2):
    bsz, n, d = x.shape
    tm = TM_ODD
    hb = tm // HALO
    nh = n // HALO
    nt = n // tm
    total = bsz * nt

    def cur(t):
        u = jnp.minimum(t, total - 1)
        return u // nt, u % nt

    def lag(t):
        u = jnp.maximum(t - 1, 0)
        return u // nt, u % nt

    def halo_lo(t):
        b, i = cur(t)
        return b, jnp.maximum(i * hb - 1, 0), 0

    def halo_hi(t):
        b, i = cur(t)
        return b, jnp.minimum((i + 1) * hb, nh - 1), 0

    return pl.pallas_call(
        functools.partial(_odd_kernel, tiles_per_seq=nt),
        grid=(total + 1,),
        in_specs=[
            pl.BlockSpec((1, HALO, d), halo_lo),
            pl.BlockSpec((1, tm, d), lambda t: (*cur(t), 0)),
            pl.BlockSpec((1, HALO, d), halo_hi),
            pl.BlockSpec((1, tm, d), lambda t: (*lag(t), 0)),
            pl.BlockSpec((1, 8, d), lambda t: (cur(t)[0], 0, 0)),
            pl.BlockSpec((1, 8, d), lambda t: (lag(t)[0], 0, 0)),
            _const_spec((1, LANES)),
            _const_spec((1, d)),
            _const_spec(win.shape),
            _const_spec(dw.shape), _const_spec(dwb.shape), _const_spec(lng.shape), _const_spec(lnb.shape),
            _const_spec(scw.shape),
            _const_spec(wout.shape),
            _const_spec((1, d)),
            _const_spec(w1.shape), _const_spec(w3.shape), _const_spec(w2.shape),
        ],
        out_specs=pl.BlockSpec((1, tm, d), lambda t: (*lag(t), 0)),
        out_shape=jax.ShapeDtypeStruct((bsz, n, d), F32),
        scratch_shapes=[
            pltpu.VMEM((tm + 2 * HALO, d), BF16),
            pltpu.VMEM((tm + 2 * HALO, CONF_W), F32),
            pltpu.VMEM((tm + 2 * HALO, SC_W), F32),
            pltpu.VMEM((tm, SC_W), F32),
            pltpu.VMEM((tm, CONF_W + SC_W), BF16),
            pltpu.VMEM((tm, d), F32),
        ],
        compiler_params=_params(("arbitrary",)),
        name="odd_layer",
    )(x, x, x, x, mod, mod, jnp.zeros((1, LANES), jnp.int32), g1n, win, dw, dwb, lng, lnb, scw, wout, g2n,
      w1, w3, w2)


def _dft_tables():
    r = FFT_R
    gw = FNET_GROUP_W
    idx = np.arange(gw)
    ang = 2.0 * np.pi * ((idx[:, None] * idx[None, :]) % gw) / gw
    cs = np.concatenate([np.cos(ang), -np.sin(ang)], axis=1) / np.sqrt(gw)
    a = np.arange(r)
    nb = DFT_NB
    eye = np.eye(nb)
    ang1 = 2.0 * np.pi * ((a[:, None] * a[None, :]) % r) / r
    fr, fi = np.cos(ang1) / 8.0, -np.sin(ang1) / 8.0
    a_re = np.concatenate([np.kron(fr, eye), np.kron(fi, eye)], axis=0)
    a_im = np.concatenate([np.kron(-fi, eye), np.kron(fr, eye)], axis=0)
    n = r * r
    c = a[:, None, None]
    dd = a[None, :, None]
    bb = a[None, None, :]
    ang2 = 2.0 * np.pi * ((bb * (c + r * dd)) % n) / n
    gr, gi_neg = np.cos(ang2) / 8.0, np.sin(ang2) / 8.0
    b_re = np.zeros((r // nb, r, nb, nb, r))
    b_im = np.zeros((r // nb, r, nb, nb, r))
    for ci in range(nb):
        b_re[:, :, ci, ci, :] = gr[ci::nb]
        b_im[:, :, ci, ci, :] = gi_neg[ci::nb]
    b_re = b_re.reshape(r // nb, r * nb, nb * r)
    b_im = b_im.reshape(r // nb, r * nb, nb * r)
    named = (("cs", cs), ("a_re", a_re), ("a_im", a_im), ("b_re", b_re), ("b_im", b_im))
    return {name: jnp.asarray(t, F32).astype(BF16) for name, t in named}


def _rope_tables(n):
    rows = n // GRID_W
    row = jnp.broadcast_to(jnp.arange(rows, dtype=F32)[:, None], (rows, GRID_W)).reshape(n)
    col = jnp.broadcast_to(jnp.arange(GRID_W, dtype=F32)[None, :], (rows, GRID_W)).reshape(n)
    per_axis = QK_ROPE // 4
    inv_freq = ROPE_BASE ** (-jnp.arange(per_axis, dtype=F32) / per_axis)
    ang = jnp.concatenate([row[:, None] * inv_freq, col[:, None] * inv_freq], axis=-1)
    cos, sin = jnp.cos(ang), jnp.sin(ang)
    tail = LANES - QK_HEAD
    rc = jnp.concatenate([jnp.ones((n, QK_NOPE), F32), cos, cos, jnp.ones((n, tail), F32)], axis=-1)
    rs = jnp.concatenate([jnp.zeros((n, QK_NOPE), F32), -sin, sin, jnp.zeros((n, tail), F32)], axis=-1)
    return rc, rs


def _swap_rope_halves(t):
    r1 = t[..., QK_NOPE:QK_NOPE + ROPE_HALF]
    r2 = t[..., QK_NOPE + ROPE_HALF:]
    return jnp.concatenate([jnp.zeros_like(t[..., :QK_NOPE]), r2, r1], axis=-1)


def _pad_heads(w, width):
    r = w.shape[0]
    return jnp.pad(w, ((0, 0), (0, 0), (0, HEAD_PAD - width))).reshape(r, MLA_HEADS * HEAD_PAD)


def _pad_gain(g):
    return jnp.pad(g, (0, HEAD_PAD - QK_HEAD)).reshape(1, HEAD_PAD)


def _pad_values(w):
    r = w.shape[0]
    pairs = w.reshape(r, MLA_HEADS // 2, 2, V_HEAD)
    zero = jnp.zeros_like(pairs[:, :, 0])
    even = jnp.concatenate([pairs[:, :, 0], zero], axis=-1)
    odd = jnp.concatenate([zero, pairs[:, :, 1]], axis=-1)
    return jnp.stack([even, odd], axis=2).reshape(r, MLA_HEADS * HEAD_PAD)


def _value_ones_row():
    row = np.zeros((MLA_HEADS, HEAD_PAD), np.float32)
    row[0::2, V_HEAD] = 1.0
    row[1::2, 0] = 1.0
    return jnp.asarray(row.reshape(1, MLA_HEADS * HEAD_PAD))


def kernel(x, c, ctx, c_ctx, ada_w, ada_b, norm1_g, norm2_g, ffn_w1, ffn_w3, ffn_w2, a_w_in, a_q_ln_g, a_kv_ln_g, a_w_uq, a_w_uk, a_w_uv, a_q_norm_g, a_k_norm_g, a_w_out, b_w_in, b_conf_dw, b_conf_dw_b, b_conf_ln_g, b_conf_ln_b, b_sc_dw, b_w_out):
    bsz, n, d = x.shape
    depth = ada_w.shape[0]
    assert depth == 2 and d == D_MODEL and n == FFT_R * FFT_R and ctx.shape[1] == CTX_LEN

    rows = 16
    cvec = jnp.concatenate([c, c_ctx[None, :], jnp.zeros((rows - bsz - 1, d), F32)], axis=0)
    ada = _ada_call(cvec, ada_w, ada_b).reshape(depth, rows, 6, d)
    unit = jnp.array([0.0, 1.0, 0.0, 0.0, 1.0, 0.0], F32)[None, None, :, None]
    mods = jnp.pad(ada + unit, ((0, 0), (0, 0), (0, 2), (0, 0)))

    tables = _dft_tables()
    tables["rc"], tables["rs"] = _rope_tables(n)
    tables["ones"] = jnp.ones((LANES, LANES), BF16)

    w_in = a_w_in[0]
    kr_cols = jnp.pad(w_in[:, Q_LORA + KV_LORA:EVEN_KV_END], ((0, 0), (QK_NOPE, LANES - QK_HEAD)))
    w = {
        "wq": jnp.concatenate([w_in[:, :Q_LORA], kr_cols], axis=1).astype(BF16),
        "wkv": w_in[:, Q_LORA:Q_LORA + KV_LORA].astype(BF16),
        "wf": w_in[:, EVEN_KV_END:].astype(BF16),
        "qlg": a_q_ln_g[0].reshape(1, Q_LORA),
        "kvlg": a_kv_ln_g[0].reshape(1, KV_LORA),
        "wuq": _pad_heads(a_w_uq[0], QK_HEAD).astype(BF16),
        "wuqs": _pad_heads(_swap_rope_halves(a_w_uq[0]), QK_HEAD).astype(BF16),
        "wuk": _pad_heads(a_w_uk[0], QK_NOPE).astype(BF16),
        "wuv": _pad_values(a_w_uv[0]).astype(BF16),
        "vone": _value_ones_row(),
        "gq": _pad_gain(a_q_norm_g[0]),
        "gqs": _pad_gain(_swap_rope_halves(a_q_norm_g[0])),
        "gk": _pad_gain(a_k_norm_g[0]),
    }
    g1 = norm1_g[0].reshape(1, d)
    q, k, v, z = _even_in_call(x, mods[0, :bsz], g1, w, tables)
    k_ctx, v_ctx = _ctx_in_call(ctx, mods[0, bsz:bsz + 1], g1, w, tables)
    y = _seq_dft_call(z, tables)
    o = _attn_call(q, k_ctx, k, v_ctx, v)
    x = _even_out_call(x, o, y, mods[0, :bsz], norm2_g[0].reshape(1, d), a_w_out[0].astype(BF16),
                       ffn_w1[0].astype(BF16), ffn_w3[0].astype(BF16), ffn_w2[0].astype(BF16))

    x = _odd_call(x, mods[1, :bsz], norm1_g[1].reshape(1, d), b_w_in[0].astype(BF16),
                  b_conf_dw[0], b_conf_dw_b[0].reshape(1, CONF_W), b_conf_ln_g[0].reshape(1, CONF_W),
                  b_conf_ln_b[0].reshape(1, CONF_W), b_sc_dw[0], b_w_out[0].astype(BF16),
                  norm2_g[1].reshape(1, d), ffn_w1[1].astype(BF16), ffn_w3[1].astype(BF16),
                  ffn_w2[1].astype(BF16))
    return x
```
